```python
import math
import jax
import jax.numpy as jnp
from jax import lax
import numpy as np

D_MODEL = 1024
BATCH = 8
SEQ = 2048
DEPTH = 4

DN_HEADS = 4
DN_DK = 128
DN_DV = 128
DN_CONV = 4
DN_CHUNK = 64
DN_QK = DN_HEADS * DN_DK
DN_WIDTH = DN_HEADS * DN_DV
SA_HEADS = 8
SA_DK = 64
SA_DV = 64
SA_Q_RANK = 256
SA_KV_RANK = 128
SA_WIDTH = SA_HEADS * SA_DV
IDX_HEADS = 8
IDX_DIM = 64
TOPK_MAX = 256
Q_BLOCK = 128
MIX_WIDTH = DN_WIDTH + SA_WIDTH
IN_SIZES = (DN_QK, DN_QK, DN_WIDTH, DN_WIDTH, DN_HEADS, DN_HEADS, SA_Q_RANK, SA_KV_RANK, IDX_DIM, IDX_HEADS)
IN_WIDTH = sum(IN_SIZES)
N_EXPERTS = 32
TOP_K = 4
D_FF = 1024
SWIGLU_LIMIT = 7.0
SWIGLU_ALPHA = 1.702
EPS = 1e-6
DEEPNORM_ALPHA = (2 * DEPTH) ** 0.25
DEEPNORM_BETA = (8 * DEPTH) ** -0.25

kernel_name = 'hybrid_deltanet_dsa_moe_deepnorm'


def _rmsnorm(x, g):
    xf = x.astype(jnp.float32)
    y = xf * lax.rsqrt(jnp.mean(xf * xf, axis=-1, keepdims=True) + EPS)
    return (y * g.astype(jnp.float32)).astype(x.dtype)


def _layernorm(x, g, b):
    xf = x.astype(jnp.float32)
    mu = jnp.mean(xf, axis=-1, keepdims=True)
    var = jnp.mean(jnp.square(xf - mu), axis=-1, keepdims=True)
    y = (xf - mu) * lax.rsqrt(var + EPS)
    return (y * g.astype(jnp.float32) + b.astype(jnp.float32)).astype(x.dtype)


def _l2norm(x):
    xf = x.astype(jnp.float32)
    return xf * lax.rsqrt(jnp.sum(xf * xf, axis=-1, keepdims=True) + EPS)


def _causal_dwconv(x, w):
    c = x.shape[-1]
    kw = w.shape[0]
    return lax.conv_general_dilated(
        x, w[:, None, :].astype(x.dtype), window_strides=(1,), padding=[(kw - 1, 0)],
        dimension_numbers=('NWC', 'WIO', 'NWC'), feature_group_count=c)


def _gated_delta_rule(q, k, v, g, beta):
    f32 = jnp.float32
    b, t, h, dk = q.shape
    dv = v.shape[-1]
    c = DN_CHUNK
    n = t // c

    def chunks(a):
        a = a.astype(f32).reshape((b, n, c, h) + a.shape[3:])
        return jnp.moveaxis(a, 3, 1)

    q = chunks(q) * (dk ** -0.5)
    k = chunks(k)
    v = chunks(v)
    g = jnp.cumsum(chunks(g), axis=-1)
    beta = chunks(beta)
    k_beta = k * beta[..., None]
    v_beta = v * beta[..., None]
    tril = jnp.tril(jnp.ones((c, c), bool))
    strict = jnp.tril(jnp.ones((c, c), bool), -1)
    decay = jnp.exp(jnp.where(tril, g[..., :, None] - g[..., None, :], -jnp.inf))
    lower = jnp.where(strict, jnp.einsum('bhncd,bhnsd->bhncs', k_beta, k) * decay, 0.0)
    eye = jnp.eye(c, dtype=f32)
    rhs = jnp.concatenate([v_beta, k_beta * jnp.exp(g)[..., None]], axis=-1)
    sol = lax.linalg.triangular_solve(eye + lower, rhs, left_side=True, lower=True, unit_diagonal=True)
    u, w = sol[..., :dv], sol[..., dv:]
    attn_intra = jnp.einsum('bhncd,bhnsd->bhncs', q, k) * decay
    g_last = g[..., -1]
    k_dec = k * jnp.exp(g_last[..., None] - g)[..., None]
    q_dec = q * jnp.exp(g)[..., None]

    def step(state, xs):
        qd, kd, uc, wc, a, gl = xs
        v_new = uc - jnp.einsum('bhck,bhkv->bhcv', wc, state)
        o = jnp.einsum('bhck,bhkv->bhcv', qd, state) + jnp.einsum('bhcs,bhsv->bhcv', a, v_new)
        state = state * jnp.exp(gl)[..., None, None] + jnp.einsum('bhck,bhcv->bhkv', kd, v_new)
        return state, o

    xs = tuple(jnp.moveaxis(a, 2, 0) for a in (q_dec, k_dec, u, w, attn_intra, g_last))
    s0 = jnp.zeros((b, h, dk, dv), f32)
    _, o = lax.scan(step, s0, xs)
    return jnp.transpose(o, (1, 0, 3, 2, 4)).reshape(b, t, h, dv)


def _gated_deltanet(dq, dk, dv, dz, db, da, conv_w, a_log, dt_bias, norm_w):
    b, t, _ = dq.shape
    qkv = jax.nn.silu(_causal_dwconv(jnp.concatenate([dq, dk, dv], axis=-1), conv_w))
    q, k, v = jnp.split(qkv, [DN_QK, 2 * DN_QK], axis=-1)
    q = _l2norm(q.reshape(b, t, DN_HEADS, DN_DK))
    k = _l2norm(k.reshape(b, t, DN_HEADS, DN_DK))
    v = v.reshape(b, t, DN_HEADS, DN_DV)
    beta = jax.nn.sigmoid(db.astype(jnp.float32))
    g = -jnp.exp(a_log.astype(jnp.float32)) * jax.nn.softplus(da.astype(jnp.float32) + dt_bias.astype(jnp.float32))
    o = _gated_delta_rule(q, k, v, g, beta)
    z = dz.reshape(b, t, DN_HEADS, DN_DV).astype(jnp.float32)
    o = _rmsnorm(o, norm_w) * jax.nn.silu(z)
    return o.reshape(b, t, DN_WIDTH).astype(dq.dtype)


def _dsa(c_q, c_kv, idx_k, idx_w, q_norm, w_uq, kv_norm, w_uk, w_uv, w_qidx, idxk_g, idxk_b):
    b, t, _ = c_q.shape
    k_top = min(TOPK_MAX, t // 4)
    nb = t // Q_BLOCK
    cq = _rmsnorm(c_q, q_norm)
    q = (cq @ w_uq).reshape(b, t, SA_HEADS, SA_DK)
    q_lat = jnp.einsum('bthd,hdr->bthr', q, w_uk) * (SA_DK ** -0.5)
    q_idx = (cq @ w_qidx).reshape(b, t, IDX_HEADS, IDX_DIM)
    ckv = _rmsnorm(c_kv, kv_norm)
    k_idx = _layernorm(idx_k, idxk_g, idxk_b)
    w_head = idx_w * ((IDX_HEADS ** -0.5) * (IDX_DIM ** -0.5))
    key_pos = jnp.arange(t)

    def blockify(a):
        return jnp.moveaxis(a.reshape((b, nb, Q_BLOCK) + a.shape[2:]), 1, 0)

    def one_block(args):
        qi, qlat_b, qidx_b, w_b = args
        t_pos = qi * Q_BLOCK + jnp.arange(Q_BLOCK)
        causal = key_pos[None, :] <= t_pos[:, None]
        logits = jax.nn.relu(jnp.einsum('bqhd,bsd->bqhs', qidx_b, k_idx))
        score = jnp.einsum('bqh,bqhs->bqs', w_b, logits).astype(jnp.float32)
        score = jnp.where(causal[None], score, -jnp.inf)
        _, sel = lax.top_k(score, k_top)
        valid = sel <= t_pos[None, :, None]
        c_sel = jax.vmap(lambda cc, ii: cc[ii])(ckv, sel)
        s = jnp.einsum('bqhr,bqkr->bqhk', qlat_b, c_sel).astype(jnp.float32)
        s = jnp.where(valid[:, :, None, :], s, -jnp.inf)
        p = jax.nn.softmax(s, axis=-1).astype(c_sel.dtype)
        return jnp.einsum('bqhk,bqkr->bqhr', p, c_sel)

    o_lat = lax.map(one_block, (jnp.arange(nb), blockify(q_lat), blockify(q_idx), blockify(w_head)))
    o_lat = jnp.moveaxis(o_lat, 0, 1).reshape(b, t, SA_HEADS, SA_KV_RANK)
    o = jnp.einsum('bthr,hrv->bthv', o_lat, w_uv)
    return o.reshape(b, t, SA_WIDTH)


def _moe(x, router_w, router_b, w_gu, b_gu, w_dn, b_dn):
    b, t, d = x.shape
    xt = x.reshape(b * t, d)
    logits = (xt @ router_w + router_b).astype(jnp.float32)
    top_val, top_idx = lax.top_k(logits, TOP_K)
    gates = jax.nn.softmax(top_val, axis=-1)
    combine = jnp.sum(jax.nn.one_hot(top_idx, N_EXPERTS, dtype=jnp.float32) * gates[..., None], axis=1)

    def expert(acc, xs):
        wgu, bgu, wd, bd, cw = xs
        gate, up = jnp.split(xt @ wgu + bgu, 2, axis=-1)
        gate = jnp.minimum(gate, SWIGLU_LIMIT)
        up = jnp.clip(up, -SWIGLU_LIMIT, SWIGLU_LIMIT)
        hdn = (up + 1.0) * (gate * jax.nn.sigmoid(SWIGLU_ALPHA * gate))
        y = hdn @ wd + bd
        return acc + cw[:, None].astype(y.dtype) * y, None

    acc, _ = lax.scan(expert, jnp.zeros_like(xt), (w_gu, b_gu, w_dn, b_dn, combine.T))
    return acc.reshape(b, t, d)


def setup_inputs(seed: int = 0) -> dict:
    key = jax.random.key(seed)
    ks = jax.random.split(key, 32)
    f32 = jnp.float32
    L = DEPTH
    conv_ch = 2 * DN_QK + DN_WIDTH

    def nrm(k, shape, scale):
        return jax.random.normal(k, shape, f32) * scale

    dt = jnp.exp(jax.random.uniform(ks[4], (L, DN_HEADS), f32, math.log(1e-3), math.log(1e-1)))
    return {
        'x': nrm(ks[0], (BATCH, SEQ, D_MODEL), 1.0),
        'w_in': nrm(ks[1], (L, D_MODEL, IN_WIDTH), D_MODEL ** -0.5),
        'dn_conv': nrm(ks[2], (L, DN_CONV, conv_ch), DN_CONV ** -0.5),
        'dn_a_log': jnp.log(jax.random.uniform(ks[3], (L, DN_HEADS), f32, 1.0, 16.0)),
        'dn_dt_bias': dt + jnp.log(-jnp.expm1(-dt)),
        'dn_norm': 1.0 + nrm(ks[5], (L, DN_DV), 0.02),
        'sa_q_norm': 1.0 + nrm(ks[6], (L, SA_Q_RANK), 0.02),
        'sa_w_uq': nrm(ks[7], (L, SA_Q_RANK, SA_HEADS * SA_DK), SA_Q_RANK ** -0.5),
        'sa_kv_norm': 1.0 + nrm(ks[8], (L, SA_KV_RANK), 0.02),
        'sa_w_uk': nrm(ks[9], (L, SA_HEADS, SA_DK, SA_KV_RANK), SA_KV_RANK ** -0.5),
        'sa_w_uv': nrm(ks[10], (L, SA_HEADS, SA_KV_RANK, SA_DV), SA_KV_RANK ** -0.5),
        'idx_w_q': nrm(ks[11], (L, SA_Q_RANK, IDX_HEADS * IDX_DIM), SA_Q_RANK ** -0.5),
        'idx_k_norm_g': 1.0 + nrm(ks[12], (L, IDX_DIM), 0.02),
        'idx_k_norm_b': nrm(ks[13], (L, IDX_DIM), 0.02),
        'w_o': nrm(ks[14], (L, MIX_WIDTH, D_MODEL), (MIX_WIDTH ** -0.5) * DEEPNORM_BETA),
        'ln1_g': 1.0 + nrm(ks[15], (L, D_MODEL), 0.02),
        'ln1_b': nrm(ks[16], (L, D_MODEL), 0.02),
        'router_w': nrm(ks[17], (L, D_MODEL, N_EXPERTS), D_MODEL ** -0.5),
        'router_b': nrm(ks[18], (L, N_EXPERTS), 0.01),
        'w_gate_up': nrm(ks[19], (L, N_EXPERTS, D_MODEL, 2 * D_FF), D_MODEL ** -0.5),
        'b_gate_up': nrm(ks[20], (L, N_EXPERTS, 2 * D_FF), 0.02),
        'w_down': nrm(ks[21], (L, N_EXPERTS, D_FF, D_MODEL), (D_FF ** -0.5) * DEEPNORM_BETA),
        'b_down': nrm(ks[22], (L, N_EXPERTS, D_MODEL), 0.02),
        'ln2_g': 1.0 + nrm(ks[23], (L, D_MODEL), 0.02),
        'ln2_b': nrm(ks[24], (L, D_MODEL), 0.02),
    }


def reference(x, w_in, dn_conv, dn_a_log, dn_dt_bias, dn_norm, sa_q_norm, sa_w_uq, sa_kv_norm, sa_w_uk, sa_w_uv, idx_w_q, idx_k_norm_g, idx_k_norm_b, w_o, ln1_g, ln1_b, router_w, router_b, w_gate_up, b_gate_up, w_down, b_down, ln2_g, ln2_b):
    splits = [int(s) for s in np.cumsum(IN_SIZES)[:-1]]
    for l in range(DEPTH):
        proj = x @ w_in[l]
        dq, dk, dv, dz, db, da, cq, ckv, ik, iw = jnp.split(proj, splits, axis=-1)
        y_dn = _gated_deltanet(dq, dk, dv, dz, db, da, dn_conv[l], dn_a_log[l], dn_dt_bias[l], dn_norm[l])
        y_sa = _dsa(cq, ckv, ik, iw, sa_q_norm[l], sa_w_uq[l], sa_kv_norm[l], sa_w_uk[l], sa_w_uv[l],
                    idx_w_q[l], idx_k_norm_g[l], idx_k_norm_b[l])
        mix = jnp.concatenate([y_dn.astype(x.dtype), y_sa.astype(x.dtype)], axis=-1) @ w_o[l]
        x = _layernorm(DEEPNORM_ALPHA * x + mix, ln1_g[l], ln1_b[l])
        ffn = _moe(x, router_w[l], router_b[l], w_gate_up[l], b_gate_up[l], w_down[l], b_down[l])
        x = _layernorm(DEEPNORM_ALPHA * x + ffn, ln2_g[l], ln2_b[l])
    return x
```

```python
import functools

import jax
import jax.numpy as jnp
from jax import lax
from jax.experimental import pallas as pl
from jax.experimental.pallas import tpu as pltpu

F32 = jnp.float32
BF16 = jnp.bfloat16
I32 = jnp.int32

D_MODEL = 1024
DN_HEADS = 4
DN_DK = 128
DN_CONV = 4
DN_QK = DN_HEADS * DN_DK
DN_WIDTH = DN_HEADS * DN_DK
SA_HEADS = 8
SA_DK = 64
SA_DV = 64
SA_Q_RANK = 256
SA_KV_RANK = 128
SA_WIDTH = SA_HEADS * SA_DV
IDX_HEADS = 8
IDX_DIM = 64
TOPK_MAX = 256
N_EXPERTS = 32
TOP_K = 4
D_FF = 1024
SWIGLU_LIMIT = 7.0
SWIGLU_ALPHA = 1.702
EPS = 1e-6
DEPTH = 4
DEEPNORM_ALPHA = (2 * DEPTH) ** 0.25

LANE = 128
DN_CHUNK = 128
PROJ_WIDTH = 2560
COL_Z = 3 * DN_QK
COL_CQ = COL_Z + DN_WIDTH
COL_CKV = COL_CQ + SA_Q_RANK
COL_MISC = COL_CKV + SA_KV_RANK
MISC_IW = IDX_DIM
MISC_DB = MISC_IW + IDX_HEADS
MISC_DA = MISC_DB + DN_HEADS
ROW_TILE = 512
Q_TILE = 128
MOE_ROWS = 256
INT_MIN = -(2 ** 31)
VMEM_LIMIT = 56 * 1024 * 1024

_HI = lax.Precision.HIGHEST


def _dot(a, b, precision=None):
    return jnp.dot(a, b, preferred_element_type=F32, precision=precision)


def _dot_nt(a, b, precision=None):
    return lax.dot_general(a, b, (((1,), (1,)), ((), ())), preferred_element_type=F32,
                           precision=precision)


def _sigmoid(x):
    return 1.0 / (1.0 + jnp.exp(-x))


def _silu(x):
    return x * _sigmoid(x)


def _softplus(x):
    return jnp.maximum(x, 0.0) + jnp.log(1.0 + jnp.exp(-jnp.abs(x)))


def _layernorm(x, g, b):
    mu = jnp.mean(x, axis=-1, keepdims=True)
    xc = x - mu
    var = jnp.mean(xc * xc, axis=-1, keepdims=True)
    return xc * lax.rsqrt(var + EPS) * g + b


def _rmsnorm(x, g):
    return x * lax.rsqrt(jnp.mean(x * x, axis=-1, keepdims=True) + EPS) * g


def _params(sem):
    return pltpu.CompilerParams(dimension_semantics=sem, vmem_limit_bytes=VMEM_LIMIT)


def _inproj_kernel(x_ref, w_ref, o_ref):
    o_ref[...] = _dot(x_ref[...].astype(BF16), w_ref[...])


def _inproj(x2, w):
    n = x2.shape[0]
    return pl.pallas_call(
        _inproj_kernel,
        grid=(n // ROW_TILE,),
        in_specs=[pl.BlockSpec((ROW_TILE, D_MODEL), lambda i: (i, 0)),
                  pl.BlockSpec((D_MODEL, PROJ_WIDTH), lambda i: (0, 0))],
        out_specs=pl.BlockSpec((ROW_TILE, PROJ_WIDTH), lambda i: (i, 0)),
        out_shape=jax.ShapeDtypeStruct((n, PROJ_WIDTH), F32),
        compiler_params=_params(("parallel",)),
        name="inproj",
    )(x2, w)


def _dn_kernel(sm_ref, q_ref, k_ref, v_ref, z_ref, g_ref, wq_ref, wk_ref, wv_ref, nw_ref,
               o_ref, qn_s, kn_s, vv_s, beta_s, gcum_s):
    h = pl.program_id(1)
    t = q_ref.shape[0]
    n_chunks = t // DN_CHUNK
    c = DN_CHUNK
    row_t = lax.broadcasted_iota(I32, (t, LANE), 0)

    def conv_silu(x_ref, w_ref):
        x = x_ref[...]
        w = w_ref[...]
        y = x * w[DN_CONV - 1:DN_CONV, :]
        for s in range(1, DN_CONV):
            xs = jnp.where(row_t >= s, pltpu.roll(x, s, axis=0), 0.0)
            y = y + xs * w[DN_CONV - 1 - s:DN_CONV - s, :]
        return _silu(y)

    def l2norm(x):
        return x * lax.rsqrt(jnp.sum(x * x, axis=-1, keepdims=True) + EPS)

    qn_s[...] = l2norm(conv_silu(q_ref, wq_ref)) * (DN_DK ** -0.5)
    kn_s[...] = l2norm(conv_silu(k_ref, wk_ref))
    vv_s[...] = conv_silu(v_ref, wv_ref)

    ii = lax.broadcasted_iota(I32, (c, c), 0)
    jj = lax.broadcasted_iota(I32, (c, c), 1)
    a_coef = jnp.exp(jnp.full((1, LANE), sm_ref[0, h], F32))
    dt_bias = sm_ref[1, h]
    beta_s[...] = _sigmoid(g_ref[0])
    g_log = -a_coef * _softplus(g_ref[1] + dt_bias)
    gcum_s[...] = _dot(g_log, jnp.where(ii <= jj, 1.0, 0.0).astype(F32), precision=_HI)

    lower_incl = ii >= jj
    nw = nw_ref[...]

    def chunk(ci, state):
        r0 = pl.multiple_of(ci * c, c)
        g_row = jnp.broadcast_to(gcum_s[pl.ds(ci, 1), :], (c, c))
        g_col = g_row.T
        b_col = jnp.broadcast_to(beta_s[pl.ds(ci, 1), :], (c, c)).T
        g_last = jnp.broadcast_to(g_row[:, c - 1:c], (c, c))
        decay = jnp.exp(jnp.where(lower_incl, g_col - g_row, -jnp.inf))
        qc = qn_s[pl.ds(r0, c), :]
        kc = kn_s[pl.ds(r0, c), :]
        vc = vv_s[pl.ds(r0, c), :]
        kb = kc * b_col
        a_pow = jnp.where(ii > jj, -(_dot_nt(kb, kc, precision=_HI) * decay), 0.0)
        sol = jnp.concatenate([vc * b_col, kb * jnp.exp(g_col)], axis=1)
        n_steps = (c - 1).bit_length()
        for step in range(n_steps):
            sol = sol + _dot(a_pow, sol, precision=_HI)
            if step + 1 < n_steps:
                a_pow = _dot(a_pow, a_pow, precision=_HI)
        u = sol[:, :c]
        w = sol[:, c:]
        attn = _dot_nt(qc.astype(BF16), kc.astype(BF16)) * decay
        k_dec_t = (kc * jnp.exp(g_last - g_col)).T
        q_dec = qc * jnp.exp(g_col)
        sb = state.astype(BF16)
        v_new = u - _dot(w.astype(BF16), sb)
        o = _dot(q_dec.astype(BF16), sb) + _dot(attn.astype(BF16), v_new.astype(BF16))
        state = state * jnp.exp(g_last) + _dot(k_dec_t.astype(BF16), v_new.astype(BF16))
        zc = z_ref[pl.ds(r0, c), :]
        o_ref[pl.ds(r0, c), :] = _rmsnorm(o, nw) * _silu(zc)
        return state

    lax.fori_loop(0, n_chunks, chunk, jnp.zeros((c, c), F32))


def _deltanet(proj, gates, dn_sm, conv_w, norm_w, batch, t):
    n = proj.shape[0]
    nc = t // DN_CHUNK
    blk = lambda off: pl.BlockSpec((t, LANE), lambda b, h: (b, off + h))
    wblk = lambda off: pl.BlockSpec((DN_CONV, LANE), lambda b, h: (0, off + h))
    return pl.pallas_call(
        _dn_kernel,
        grid=(batch, DN_HEADS),
        in_specs=[pl.BlockSpec(memory_space=pltpu.SMEM),
                  blk(0), blk(DN_HEADS), blk(2 * DN_HEADS), blk(3 * DN_HEADS),
                  pl.BlockSpec((None, None, 2, nc, LANE), lambda b, h: (b, h, 0, 0, 0)),
                  wblk(0), wblk(DN_HEADS), wblk(2 * DN_HEADS),
                  pl.BlockSpec((1, LANE), lambda b, h: (0, 0))],
        out_specs=pl.BlockSpec((t, LANE), lambda b, h: (b, h)),
        out_shape=jax.ShapeDtypeStruct((n, DN_WIDTH), F32),
        scratch_shapes=[pltpu.VMEM((t, LANE), F32), pltpu.VMEM((t, LANE), F32),
                        pltpu.VMEM((t, LANE), F32), pltpu.VMEM((nc, LANE), F32),
                        pltpu.VMEM((nc, LANE), F32)],
        compiler_params=_params(("parallel", "parallel")),
        name="deltanet",
    )(dn_sm, proj, proj, proj, proj, gates, conv_w, conv_w, conv_w, norm_w)


def _sa_prep_kernel(cq_ref, ckv_ref, misc_ref, qn_ref, wuq_ref, wuk_ref, wqi_ref, kvn_ref,
                    ig_ref, ib_ref, qlat_ref, qidx_ref, ckvn_ref, kidx_ref):
    cq = _rmsnorm(cq_ref[...], qn_ref[...]).astype(BF16)
    q = _dot(cq, wuq_ref[...])
    qi = _dot(cq, wqi_ref[...])
    for h in range(SA_HEADS):
        qh = q[:, h * SA_DK:(h + 1) * SA_DK].astype(BF16)
        ql = _dot(qh, wuk_ref[h]) * (SA_DK ** -0.5)
        qlat_ref[:, h * SA_KV_RANK:(h + 1) * SA_KV_RANK] = ql.astype(BF16)
    for h in range(IDX_HEADS):
        qidx_ref[h] = qi[:, h * IDX_DIM:(h + 1) * IDX_DIM].astype(BF16)
    ckvn_ref[...] = _rmsnorm(ckv_ref[...], kvn_ref[...]).astype(BF16)
    ik = misc_ref[...][:, :IDX_DIM]
    kidx_ref[...] = _layernorm(ik, ig_ref[...], ib_ref[...]).astype(BF16)


def _sa_prep(proj, q_norm, w_uq, w_uk, w_qidx, kv_norm, ig, ib):
    n = proj.shape[0]
    tm = ROW_TILE
    const2 = lambda shp: pl.BlockSpec(shp, lambda i: (0, 0))
    return pl.pallas_call(
        _sa_prep_kernel,
        grid=(n // tm,),
        in_specs=[pl.BlockSpec((tm, SA_Q_RANK), lambda i: (i, COL_CQ // SA_Q_RANK)),
                  pl.BlockSpec((tm, SA_KV_RANK), lambda i: (i, COL_CKV // SA_KV_RANK)),
                  pl.BlockSpec((tm, LANE), lambda i: (i, COL_MISC // LANE)),
                  const2((1, SA_Q_RANK)),
                  const2((SA_Q_RANK, SA_HEADS * SA_DK)),
                  pl.BlockSpec((SA_HEADS, SA_DK, SA_KV_RANK), lambda i: (0, 0, 0)),
                  const2((SA_Q_RANK, IDX_HEADS * IDX_DIM)),
                  const2((1, SA_KV_RANK)),
                  const2((1, IDX_DIM)), const2((1, IDX_DIM))],
        out_specs=[pl.BlockSpec((tm, SA_HEADS * SA_KV_RANK), lambda i: (i, 0)),
                   pl.BlockSpec((IDX_HEADS, tm, IDX_DIM), lambda i: (0, i, 0)),
                   pl.BlockSpec((tm, SA_KV_RANK), lambda i: (i, 0)),
                   pl.BlockSpec((tm, IDX_DIM), lambda i: (i, 0))],
        out_shape=[jax.ShapeDtypeStruct((n, SA_HEADS * SA_KV_RANK), BF16),
                   jax.ShapeDtypeStruct((IDX_HEADS, n, IDX_DIM), BF16),
                   jax.ShapeDtypeStruct((n, SA_KV_RANK), BF16),
                   jax.ShapeDtypeStruct((n, IDX_DIM), BF16)],
        compiler_params=_params(("parallel",)),
        name="sa_prep",
    )(proj, proj, proj, q_norm, w_uq, w_uk, w_qidx, kv_norm, ig, ib)


def _sa_kernel(qidx_ref, qlat_ref, misc_ref, kidx_ref, ckv_ref, o_ref, *, k_top):
    qb = qlat_ref.shape[0]
    t = kidx_ref.shape[0]
    j = pl.program_id(1)
    kidx = kidx_ref[...]
    misc = misc_ref[...]
    w_scale = (IDX_HEADS ** -0.5) * (IDX_DIM ** -0.5)
    score = jnp.zeros((qb, t), F32)
    for h in range(IDX_HEADS):
        logit = jnp.maximum(_dot_nt(qidx_ref[h], kidx), 0.0)
        score = score + (misc[:, MISC_IW + h:MISC_IW + h + 1] * w_scale) * logit

    t_pos = j * qb + lax.broadcasted_iota(I32, (qb, t), 0)
    key_pos = lax.broadcasted_iota(I32, (qb, t), 1)
    causal = key_pos <= t_pos
    bits = pltpu.bitcast(score, I32)
    key = bits ^ ((bits >> 31) & 0x7FFFFFFF)
    key = jnp.where(causal, key, INT_MIN)

    def count(pred):
        return jnp.sum(jnp.where(pred, 1, 0).astype(I32), axis=1, keepdims=True)

    def thr_step(i, thr_u):
        cand_u = thr_u | lax.shift_left(jnp.int32(1), 31 - i)
        cnt = count(key >= (cand_u ^ INT_MIN))
        return jnp.where(cnt >= k_top, cand_u, thr_u)

    thr = lax.fori_loop(0, 32, thr_step, jnp.zeros((qb, 1), I32)) ^ INT_MIN
    above = key > thr
    tie = jnp.where(causal, jnp.where(key == thr, 1, 0), 0).astype(I32)
    need = k_top - count(above)

    def tie_step(i, bound):
        cand = bound | lax.shift_left(jnp.int32(1), (t.bit_length() - 1) - i)
        cnt = jnp.sum(jnp.where(key_pos < cand, tie, 0), axis=1, keepdims=True)
        return jnp.where(cnt <= need, cand, bound)

    bound = lax.fori_loop(0, t.bit_length(), tie_step, jnp.zeros((qb, 1), I32))
    tie_sel = jnp.where(key_pos < bound, tie, 0)
    bias = jnp.where(above, 0.0, jnp.where(tie_sel > 0, 0.0, -jnp.inf)).astype(F32)

    ckv = ckv_ref[...]
    for h in range(SA_HEADS):
        s = _dot_nt(qlat_ref[:, h * SA_KV_RANK:(h + 1) * SA_KV_RANK], ckv) + bias
        m = jnp.max(s, axis=1, keepdims=True)
        p = jnp.exp(s - m)
        l = jnp.sum(p, axis=1, keepdims=True)
        o = _dot(p.astype(BF16), ckv) / l
        o_ref[:, h * SA_KV_RANK:(h + 1) * SA_KV_RANK] = o.astype(BF16)


def _sparse_attention(qidx, qlat, proj, kidx, ckvn, batch, t):
    n = proj.shape[0]
    nq = t // Q_TILE
    k_top = min(TOPK_MAX, t // 4)
    return pl.pallas_call(
        functools.partial(_sa_kernel, k_top=k_top),
        grid=(batch, nq),
        in_specs=[pl.BlockSpec((IDX_HEADS, Q_TILE, IDX_DIM), lambda b, j: (0, b * nq + j, 0)),
                  pl.BlockSpec((Q_TILE, SA_HEADS * SA_KV_RANK), lambda b, j: (b * nq + j, 0)),
                  pl.BlockSpec((Q_TILE, LANE), lambda b, j: (b * nq + j, COL_MISC // LANE)),
                  pl.BlockSpec((t, IDX_DIM), lambda b, j: (b, 0)),
                  pl.BlockSpec((t, SA_KV_RANK), lambda b, j: (b, 0))],
        out_specs=pl.BlockSpec((Q_TILE, SA_HEADS * SA_KV_RANK), lambda b, j: (b * nq + j, 0)),
        out_shape=jax.ShapeDtypeStruct((n, SA_HEADS * SA_KV_RANK), BF16),
        compiler_params=_params(("parallel", "parallel")),
        name="sparse_attention",
    )(qidx, qlat, proj, kidx, ckvn)


def _outproj_kernel(ydn_ref, olat_ref, x_ref, wdn_ref, wsa_ref, wuv_ref, g_ref, b_ref, rw_ref,
                    rb_ref, x1_ref, x1b_ref, lg_ref):
    ysa = _dot(olat_ref[...], wuv_ref[...])
    mix = _dot(ydn_ref[...].astype(BF16), wdn_ref[...]) + _dot(ysa.astype(BF16), wsa_ref[...])
    x1 = _layernorm(DEEPNORM_ALPHA * x_ref[...] + mix, g_ref[...], b_ref[...])
    x1_ref[...] = x1
    x1b_ref[...] = x1.astype(BF16)
    lg_ref[...] = _dot(x1, rw_ref[...], precision=_HI) + rb_ref[...]


def _outproj(ydn, olat, x2, w_o_dn, w_o_sa, wuv_bd, g, b, rw, rb):
    n = x2.shape[0]
    tm = ROW_TILE
    row = lambda w: pl.BlockSpec((tm, w), lambda i: (i, 0))
    const2 = lambda shp: pl.BlockSpec(shp, lambda i: (0, 0))
    return pl.pallas_call(
        _outproj_kernel,
        grid=(n // tm,),
        in_specs=[row(DN_WIDTH), row(SA_HEADS * SA_KV_RANK), row(D_MODEL),
                  const2((DN_WIDTH, D_MODEL)), const2((SA_WIDTH, D_MODEL)),
                  const2((SA_HEADS * SA_KV_RANK, SA_WIDTH)),
                  const2((1, D_MODEL)), const2((1, D_MODEL)),
                  const2((D_MODEL, N_EXPERTS)), const2((1, N_EXPERTS))],
        out_specs=[row(D_MODEL), row(D_MODEL), row(N_EXPERTS)],
        out_shape=[jax.ShapeDtypeStruct((n, D_MODEL), F32),
                   jax.ShapeDtypeStruct((n, D_MODEL), BF16),
                   jax.ShapeDtypeStruct((n, N_EXPERTS), F32)],
        compiler_params=_params(("parallel",)),
        name="outproj",
    )(ydn, olat, x2, w_o_dn, w_o_sa, wuv_bd, g, b, rw, rb)


def _route_kernel(lg_ref, rt_ref, gt_ref, r_ref, cnt_ref, pad_s):
    s_tok = lg_ref.shape[0]
    l = lg_ref[...]
    lane = lax.broadcasted_iota(I32, l.shape, 1)
    vals, hots = [], []
    for _ in range(TOP_K):
        m = jnp.max(l, axis=1, keepdims=True)
        idx = jnp.min(jnp.where(l == m, lane, N_EXPERTS), axis=1, keepdims=True)
        hot = lane == idx
        vals.append(m)
        hots.append(hot)
        l = jnp.where(hot, -jnp.inf, l)
    exps = [jnp.exp(v - vals[0]) for v in vals]
    den = exps[0]
    for e in exps[1:]:
        den = den + e
    gates = jnp.zeros(l.shape, F32)
    chosen = jnp.zeros(l.shape, F32)
    for hot, e in zip(hots, exps):
        gates = gates + jnp.where(hot, e / den, 0.0)
        chosen = chosen + jnp.where(hot, 1.0, 0.0)

    blk = 256
    ii = lax.broadcasted_iota(I32, (blk, blk), 0)
    jj = lax.broadcasted_iota(I32, (blk, blk), 1)
    tri = jnp.where(ii > jj, 1.0, 0.0).astype(BF16)
    carry = jnp.zeros((1, N_EXPERTS), F32)
    for ci in range(s_tok // blk):
        mc = chosen[ci * blk:(ci + 1) * blk, :]
        rk = _dot(tri, mc.astype(BF16)) + carry
        rank = jnp.where(mc > 0.0, rk, -1.0)
        r_ref[ci * blk:(ci + 1) * blk, :] = rank.astype(I32)
        pad_s[ci * blk:(ci + 1) * blk, 0:N_EXPERTS] = rank
        pad_s[ci * blk:(ci + 1) * blk, N_EXPERTS:2 * N_EXPERTS] = gates[ci * blk:(ci + 1) * blk, :]
        carry = carry + jnp.sum(mc, axis=0, keepdims=True)
    cnt_ref[...] = carry.astype(I32)
    pad_s[:, 2 * N_EXPERTS:] = jnp.zeros((s_tok, LANE - 2 * N_EXPERTS), F32)
    tr = pad_s[...].T
    rt_ref[...] = tr[0:N_EXPERTS, :].astype(I32)
    gt_ref[...] = tr[N_EXPERTS:2 * N_EXPERTS, :]


def _route(logits, batch, s_tok):
    n = logits.shape[0]
    return pl.pallas_call(
        _route_kernel,
        grid=(batch,),
        in_specs=[pl.BlockSpec((s_tok, N_EXPERTS), lambda b: (b, 0))],
        out_specs=[pl.BlockSpec((None, N_EXPERTS, s_tok), lambda b: (b, 0, 0)),
                   pl.BlockSpec((None, N_EXPERTS, s_tok), lambda b: (b, 0, 0)),
                   pl.BlockSpec((s_tok, N_EXPERTS), lambda b: (b, 0)),
                   pl.BlockSpec((None, 1, N_EXPERTS), lambda b: (b, 0, 0))],
        out_shape=[jax.ShapeDtypeStruct((batch, N_EXPERTS, s_tok), I32),
                   jax.ShapeDtypeStruct((batch, N_EXPERTS, s_tok), F32),
                   jax.ShapeDtypeStruct((n, N_EXPERTS), I32),
                   jax.ShapeDtypeStruct((batch, 1, N_EXPERTS), I32)],
        scratch_shapes=[pltpu.VMEM((s_tok, LANE), F32)],
        compiler_params=_params(("parallel",)),
        name="route",
    )(logits)


def _moe_kernel(cnt_ref, x_ref, rt_ref, gt_ref, r_ref, wgu_ref, bgu_ref, wdn_ref, bdn_ref, o_ref):
    b = pl.program_id(0)
    e = pl.program_id(1)
    s_tok = x_ref.shape[0]
    m = MOE_ROWS

    @pl.when(e == 0)
    def _():
        o_ref[...] = jnp.zeros(o_ref.shape, F32)

    cnt = cnt_ref[b * N_EXPERTS + e]
    n_pass = (cnt + m - 1) // m
    rank_row = rt_ref[pl.ds(e, 1), :]
    gate_row = gt_ref[pl.ds(e, 1), :]
    lane_e = lax.broadcasted_iota(I32, (s_tok, N_EXPERTS), 1)
    rank_col = jnp.sum(jnp.where(lane_e == e, r_ref[...], 0), axis=1, keepdims=True)
    sub_iota = lax.broadcasted_iota(I32, (m, s_tok), 0)
    lane_iota = lax.broadcasted_iota(I32, (s_tok, m), 1)

    def one_pass(p, carry):
        r0 = p * m
        hit = (rank_row - r0) == sub_iota
        xs = _dot(jnp.where(hit, 1.0, 0.0).astype(BF16), x_ref[...]).astype(BF16)
        hgu = _dot(xs, wgu_ref[...]) + bgu_ref[...]
        gate = jnp.minimum(hgu[:, :D_FF], SWIGLU_LIMIT)
        up = jnp.clip(hgu[:, D_FF:], -SWIGLU_LIMIT, SWIGLU_LIMIT)
        hdn = (up + 1.0) * (gate * _sigmoid(SWIGLU_ALPHA * gate))
        y = _dot(hdn.astype(BF16), wdn_ref[...]) + bdn_ref[...]
        g_rows = jnp.sum(jnp.where(hit, gate_row, 0.0), axis=1, keepdims=True)
        ys = (y * g_rows).astype(BF16)
        back = jnp.where((rank_col - r0) == lane_iota, 1.0, 0.0).astype(BF16)
        o_ref[...] += _dot(back, ys)
        return carry

    lax.fori_loop(0, n_pass, one_pass, 0)


def _moe(counts, x1b, rt, gt, r, wgu, bgu, wdn, bdn, batch, s_tok):
    n = x1b.shape[0]
    return pl.pallas_call(
        _moe_kernel,
        grid_spec=pltpu.PrefetchScalarGridSpec(
            num_scalar_prefetch=1,
            grid=(batch, N_EXPERTS),
            in_specs=[pl.BlockSpec((s_tok, D_MODEL), lambda b, e, c: (b, 0)),
                      pl.BlockSpec((None, N_EXPERTS, s_tok), lambda b, e, c: (b, 0, 0)),
                      pl.BlockSpec((None, N_EXPERTS, s_tok), lambda b, e, c: (b, 0, 0)),
                      pl.BlockSpec((s_tok, N_EXPERTS), lambda b, e, c: (b, 0)),
                      pl.BlockSpec((None, D_MODEL, 2 * D_FF), lambda b, e, c: (e, 0, 0)),
                      pl.BlockSpec((None, 1, 2 * D_FF), lambda b, e, c: (e, 0, 0)),
                      pl.BlockSpec((None, D_FF, D_MODEL), lambda b, e, c: (e, 0, 0)),
                      pl.BlockSpec((None, 1, D_MODEL), lambda b, e, c: (e, 0, 0))],
            out_specs=pl.BlockSpec((s_tok, D_MODEL), lambda b, e, c: (b, 0))),
        out_shape=jax.ShapeDtypeStruct((n, D_MODEL), F32),
        compiler_params=_params(("parallel", "arbitrary")),
        name="moe",
    )(counts, x1b, rt, gt, r, wgu, bgu, wdn, bdn)


def _ln2_kernel(x_ref, f_ref, g_ref, b_ref, o_ref):
    o_ref[...] = _layernorm(DEEPNORM_ALPHA * x_ref[...] + f_ref[...], g_ref[...], b_ref[...])


def _ln2(x1, ffn, g, b):
    n = x1.shape[0]
    tm = ROW_TILE
    row = pl.BlockSpec((tm, D_MODEL), lambda i: (i, 0))
    const = pl.BlockSpec((1, D_MODEL), lambda i: (0, 0))
    return pl.pallas_call(
        _ln2_kernel,
        grid=(n // tm,),
        in_specs=[row, row, const, const],
        out_specs=row,
        out_shape=jax.ShapeDtypeStruct((n, D_MODEL), F32),
        compiler_params=_params(("parallel",)),
        name="ln2",
    )(x1, ffn, g, b)


def _permute_w_in(w_in):
    depth = w_in.shape[0]
    o_db = 4 * DN_QK
    o_da = o_db + DN_HEADS
    o_cq = o_da + DN_HEADS
    o_ckv = o_cq + SA_Q_RANK
    o_ik = o_ckv + SA_KV_RANK
    o_iw = o_ik + IDX_DIM
    o_end = o_iw + IDX_HEADS
    parts = [w_in[..., :o_db], w_in[..., o_cq:o_ckv], w_in[..., o_ckv:o_ik], w_in[..., o_ik:o_iw],
             w_in[..., o_iw:o_end], w_in[..., o_db:o_da], w_in[..., o_da:o_cq]]
    width = sum(p.shape[-1] for p in parts)
    parts.append(jnp.zeros((depth, D_MODEL, PROJ_WIDTH - width), w_in.dtype))
    return jnp.concatenate(parts, axis=-1).astype(BF16)


def kernel(x, w_in, dn_conv, dn_a_log, dn_dt_bias, dn_norm, sa_q_norm, sa_w_uq, sa_kv_norm, sa_w_uk, sa_w_uv, idx_w_q, idx_k_norm_g, idx_k_norm_b, w_o, ln1_g, ln1_b, router_w, router_b, w_gate_up, b_gate_up, w_down, b_down, ln2_g, ln2_b):
    batch, t, d = x.shape
    depth = w_in.shape[0]
    n = batch * t
    nc = t // DN_CHUNK

    w_in_p = _permute_w_in(w_in)
    w_uq = sa_w_uq.astype(BF16)
    w_uk = sa_w_uk.astype(BF16)
    w_qidx = idx_w_q.astype(BF16)
    w_o_b = w_o.astype(BF16)
    eye_h = jnp.eye(SA_HEADS, dtype=F32)
    wuv_bd = jnp.einsum('lhrv,hg->lhrgv', sa_w_uv, eye_h).reshape(
        depth, SA_HEADS * SA_KV_RANK, SA_WIDTH).astype(BF16)
    wgu = w_gate_up.astype(BF16)
    wdn = w_down.astype(BF16)
    dn_sm = jnp.stack([dn_a_log, dn_dt_bias], axis=1)

    x2 = x.reshape(n, d)
    for l in range(depth):
        proj = _inproj(x2, w_in_p[l])
        gates = proj[:, COL_MISC + MISC_DB:COL_MISC + MISC_DA + DN_HEADS]
        gates = gates.reshape(batch, nc, DN_CHUNK, 2, DN_HEADS).transpose(0, 4, 3, 1, 2)
        ydn = _deltanet(proj, gates, dn_sm[l], dn_conv[l], dn_norm[l][None, :], batch, t)
        qlat, qidx, ckvn, kidx = _sa_prep(
            proj, sa_q_norm[l][None, :], w_uq[l], w_uk[l], w_qidx[l], sa_kv_norm[l][None, :],
            idx_k_norm_g[l][None, :], idx_k_norm_b[l][None, :])
        olat = _sparse_attention(qidx, qlat, proj, kidx, ckvn, batch, t)
        x1, x1b, logits = _outproj(
            ydn, olat, x2, w_o_b[l, :DN_WIDTH], w_o_b[l, DN_WIDTH:], wuv_bd[l],
            ln1_g[l][None, :], ln1_b[l][None, :], router_w[l], router_b[l][None, :])
        rt, gt, r, cnt = _route(logits, batch, t)
        ffn = _moe(cnt.reshape(batch * N_EXPERTS), x1b, rt, gt, r, wgu[l],
                   b_gate_up[l][:, None, :], wdn[l], b_down[l][:, None, :], batch, t)
        x2 = _ln2(x1, ffn, ln2_g[l][None, :], ln2_b[l][None, :])
    return x2.reshape(batch, t, d)
```

```python
import functools

import jax
import jax.numpy as jnp
from jax import lax
from jax.experimental import pallas as pl
from jax.experimental.pallas import tpu as pltpu

F32 = jnp.float32
BF16 = jnp.bfloat16
I32 = jnp.int32

D_MODEL = 1024
DN_HEADS = 4
DN_DK = 128
DN_CONV = 4
DN_QK = DN_HEADS * DN_DK
DN_WIDTH = DN_HEADS * DN_DK
SA_HEADS = 8
SA_DK = 64
SA_DV = 64
SA_Q_RANK = 256
SA_KV_RANK = 128
SA_WIDTH = SA_HEADS * SA_DV
IDX_HEADS = 8
IDX_DIM = 64
TOPK_MAX = 256
N_EXPERTS = 32
TOP_K = 4
D_FF = 1024
SWIGLU_LIMIT = 7.0
SWIGLU_ALPHA = 1.702
EPS = 1e-6
DEPTH = 4
DEEPNORM_ALPHA = (2 * DEPTH) ** 0.25

LANE = 128
MXU_DIM = 256
DN_CHUNK = 128
DN_PAIR = 2
PROJ_WIDTH = 2560
COL_Z = 3 * DN_QK
COL_CQ = COL_Z + DN_WIDTH
COL_CKV = COL_CQ + SA_Q_RANK
COL_MISC = COL_CKV + SA_KV_RANK
MISC_IW = IDX_DIM
MISC_DB = MISC_IW + IDX_HEADS
MISC_DA = MISC_DB + DN_HEADS
ROW_TILE = 512
Q_TILE = 128
SA_CLASSES = 8
MOE_TOKENS = 1024
MOE_ROWS = 160
INT_MIN = -(2 ** 31)
VMEM_LIMIT = 56 * 1024 * 1024

_HI = lax.Precision.HIGHEST


def _dot(a, b, precision=None):
    return jnp.dot(a, b, preferred_element_type=F32, precision=precision)


def _dot_nt(a, b, precision=None):
    return lax.dot_general(a, b, (((1,), (1,)), ((), ())), preferred_element_type=F32,
                           precision=precision)


def _dot_tn(a, b):
    return lax.dot_general(a, b, (((0,), (0,)), ((), ())), preferred_element_type=F32)


def _split(x):
    hi = x.astype(BF16)
    return hi, (x - hi.astype(F32)).astype(BF16)


def _sigmoid(x):
    return 1.0 / (1.0 + jnp.exp(-x))


def _silu(x):
    return x * _sigmoid(x)


def _softplus(x):
    return jnp.maximum(x, 0.0) + jnp.log(1.0 + jnp.exp(-jnp.abs(x)))


def _layernorm(x, g, b):
    mu = jnp.mean(x, axis=-1, keepdims=True)
    xc = x - mu
    var = jnp.mean(xc * xc, axis=-1, keepdims=True)
    return xc * lax.rsqrt(var + EPS) * g + b


def _rmsnorm(x, g):
    return x * lax.rsqrt(jnp.mean(x * x, axis=-1, keepdims=True) + EPS) * g


def _params(sem):
    return pltpu.CompilerParams(dimension_semantics=sem, vmem_limit_bytes=VMEM_LIMIT)


def _inproj_kernel(x_ref, w_ref, o_ref):
    o_ref[...] = _dot(x_ref[...].astype(BF16), w_ref[...])


def _inproj(x2, w):
    n = x2.shape[0]
    return pl.pallas_call(
        _inproj_kernel,
        grid=(n // ROW_TILE,),
        in_specs=[pl.BlockSpec((ROW_TILE, D_MODEL), lambda i: (i, 0)),
                  pl.BlockSpec((D_MODEL, PROJ_WIDTH), lambda i: (0, 0))],
        out_specs=pl.BlockSpec((ROW_TILE, PROJ_WIDTH), lambda i: (i, 0)),
        out_shape=jax.ShapeDtypeStruct((n, PROJ_WIDTH), F32),
        compiler_params=_params(("parallel",)),
        name="inproj",
    )(x2, w)


def _dn_kernel(sm_ref, q_ref, k_ref, v_ref, z_ref, g_ref, wq_ref, wk_ref, wv_ref, nw_ref,
               o_ref, qn_s, kn_s, vv_s, beta_s, gcum_s, u_s, w_s, at_s, qd_s, kdt_s, gl_s):
    hp = pl.program_id(1)
    t = q_ref.shape[0]
    c = DN_CHUNK
    n_chunks = t // c
    row_t = lax.broadcasted_iota(I32, (t, LANE), 0)
    ii = lax.broadcasted_iota(I32, (c, c), 0)
    jj = lax.broadcasted_iota(I32, (c, c), 1)
    lower_incl = ii >= jj
    cum_mat = jnp.where(ii <= jj, 1.0, 0.0).astype(F32)
    nw = nw_ref[...]

    def conv_silu(x, w):
        y = x * w[DN_CONV - 1:DN_CONV, :]
        for s in range(1, DN_CONV):
            xs = jnp.where(row_t >= s, pltpu.roll(x, s, axis=0), 0.0)
            y = y + xs * w[DN_CONV - 1 - s:DN_CONV - s, :]
        return _silu(y)

    def l2norm(x):
        return x * lax.rsqrt(jnp.sum(x * x, axis=-1, keepdims=True) + EPS)

    for hh in range(DN_PAIR):
        sl = slice(hh * LANE, (hh + 1) * LANE)
        head = hp * DN_PAIR + hh
        qn_s[hh] = l2norm(conv_silu(q_ref[:, sl], wq_ref[:, sl])) * (DN_DK ** -0.5)
        kn_s[hh] = l2norm(conv_silu(k_ref[:, sl], wk_ref[:, sl]))
        vv_s[hh] = conv_silu(v_ref[:, sl], wv_ref[:, sl])
        a_coef = jnp.exp(jnp.full((1, LANE), sm_ref[0, head], F32))
        beta_s[hh] = _sigmoid(g_ref[hh, 0])
        g_log = -a_coef * _softplus(g_ref[hh, 1] + sm_ref[1, head])
        gcum_s[hh] = _dot(g_log, cum_mat, precision=_HI)

    def local(ci, carry):
        r0 = pl.multiple_of(ci * c, c)
        a_pows, sols = [], []
        for hh in range(DN_PAIR):
            g_row = jnp.broadcast_to(gcum_s[hh, pl.ds(ci, 1), :], (c, c))
            g_col = g_row.T
            b_col = jnp.broadcast_to(beta_s[hh, pl.ds(ci, 1), :], (c, c)).T
            g_last = jnp.broadcast_to(g_row[:, c - 1:c], (c, c))
            decay = jnp.exp(jnp.where(lower_incl, g_col - g_row, -jnp.inf))
            qc = qn_s[hh, pl.ds(r0, c), :]
            kc = kn_s[hh, pl.ds(r0, c), :]
            vc = vv_s[hh, pl.ds(r0, c), :]
            kb = kc * b_col
            kcb = kc.astype(BF16)
            a_pows.append(jnp.where(ii > jj, -(_dot_nt(kb.astype(BF16), kcb) * decay), 0.0))
            sols.append(jnp.concatenate([vc * b_col, kb * jnp.exp(g_col)], axis=1))
            at_s[hh, pl.ds(r0, c), :] = (_dot_nt(qc.astype(BF16), kcb) * decay).astype(BF16)
            kdt_s[hh, pl.ds(r0, c), :] = (kc * jnp.exp(g_last - g_col)).T.astype(BF16)
            qd_s[hh, pl.ds(r0, c), :] = (qc * jnp.exp(g_col)).astype(BF16)
            gl_s[hh, pl.ds(ci, 1), :] = jnp.exp(g_last[0:1, :])

        n_steps = (c - 1).bit_length()
        wide = DN_PAIR * c
        for step in range(n_steps):
            last = step + 1 == n_steps
            rhs = jnp.concatenate(
                [sols[hh] if last else jnp.concatenate([sols[hh], a_pows[hh]], axis=1)
                 for hh in range(DN_PAIR)], axis=0)
            a_bd = jnp.concatenate(
                [jnp.concatenate([a_pows[hh] if g == hh else jnp.zeros((c, c), F32)
                                  for g in range(DN_PAIR)], axis=1)
                 for hh in range(DN_PAIR)], axis=0)
            a_hi, a_lo = _split(a_bd)
            r_hi, r_lo = _split(rhs)
            both = _dot(jnp.concatenate([a_hi, a_lo], axis=0), r_hi)
            prod = both[0:wide, :] + (both[wide:, :] + _dot(a_hi, r_lo))
            for hh in range(DN_PAIR):
                mine = prod[hh * c:(hh + 1) * c, :]
                sols[hh] = sols[hh] + mine[:, 0:2 * c]
                if not last:
                    a_pows[hh] = mine[:, 2 * c:]
        for hh in range(DN_PAIR):
            u_s[hh, pl.ds(r0, c), :] = sols[hh][:, :c]
            w_s[hh, pl.ds(r0, c), :] = sols[hh][:, c:].astype(BF16)
        return carry

    lax.fori_loop(0, n_chunks, local, 0, unroll=2)

    def recur(ci, states):
        r0 = pl.multiple_of(ci * c, c)
        new_states = []
        for hh in range(DN_PAIR):
            sl = slice(hh * LANE, (hh + 1) * LANE)
            state = states[hh]
            sb = state.astype(BF16)
            v_new = u_s[hh, pl.ds(r0, c), :] - _dot(w_s[hh, pl.ds(r0, c), :], sb)
            vb = v_new.astype(BF16)
            o = _dot(qd_s[hh, pl.ds(r0, c), :], sb) + _dot(at_s[hh, pl.ds(r0, c), :], vb)
            g_last = jnp.broadcast_to(gl_s[hh, pl.ds(ci, 1), :], (c, c))
            new_states.append(state * g_last + _dot(kdt_s[hh, pl.ds(r0, c), :], vb))
            o_ref[pl.ds(r0, c), sl] = _rmsnorm(o, nw) * _silu(z_ref[pl.ds(r0, c), sl])
        return tuple(new_states)

    lax.fori_loop(0, n_chunks, recur, tuple(jnp.zeros((c, c), F32) for _ in range(DN_PAIR)))


def _deltanet(proj, gates, dn_sm, conv_w, norm_w, batch, t):
    n = proj.shape[0]
    nc = t // DN_CHUNK
    width = DN_PAIR * LANE
    n_pairs = DN_HEADS // DN_PAIR
    blk = lambda off: pl.BlockSpec((t, width), lambda b, h: (b, off + h))
    wblk = lambda off: pl.BlockSpec((DN_CONV, width), lambda b, h: (0, off + h))
    seq_f32 = pltpu.VMEM((DN_PAIR, t, LANE), F32)
    seq_bf16 = pltpu.VMEM((DN_PAIR, t, LANE), BF16)
    per_chunk = pltpu.VMEM((DN_PAIR, nc, LANE), F32)
    return pl.pallas_call(
        _dn_kernel,
        grid=(batch, n_pairs),
        in_specs=[pl.BlockSpec(memory_space=pltpu.SMEM),
                  blk(0), blk(n_pairs), blk(2 * n_pairs), blk(3 * n_pairs),
                  pl.BlockSpec((None, DN_PAIR, 2, nc, LANE), lambda b, h: (b, h, 0, 0, 0)),
                  wblk(0), wblk(n_pairs), wblk(2 * n_pairs),
                  pl.BlockSpec((1, LANE), lambda b, h: (0, 0))],
        out_specs=pl.BlockSpec((t, width), lambda b, h: (b, h)),
        out_shape=jax.ShapeDtypeStruct((n, DN_WIDTH), F32),
        scratch_shapes=[seq_f32, seq_f32, seq_f32, per_chunk, per_chunk,
                        seq_f32, seq_bf16, seq_bf16, seq_bf16, seq_bf16, per_chunk],
        compiler_params=_params(("parallel", "parallel")),
        name="deltanet",
    )(dn_sm, proj, proj, proj, proj, gates, conv_w, conv_w, conv_w, norm_w)


def _sa_prep_kernel(cq_ref, ckv_ref, misc_ref, qn_ref, wuq_ref, wuk_ref, wqi_ref, kvn_ref,
                    ig_ref, ib_ref, qlat_ref, qidx_ref, ckvn_ref, kidx_ref):
    cq = _rmsnorm(cq_ref[...], qn_ref[...]).astype(BF16)
    q = _dot(cq, wuq_ref[...])
    qi = _dot(cq, wqi_ref[...])
    for h in range(SA_HEADS):
        qh = q[:, h * SA_DK:(h + 1) * SA_DK].astype(BF16)
        ql = _dot(qh, wuk_ref[h]) * (SA_DK ** -0.5)
        qlat_ref[:, h * SA_KV_RANK:(h + 1) * SA_KV_RANK] = ql.astype(BF16)
    for h in range(IDX_HEADS):
        qidx_ref[h] = qi[:, h * IDX_DIM:(h + 1) * IDX_DIM].astype(BF16)
    ckvn_ref[...] = _rmsnorm(ckv_ref[...], kvn_ref[...]).astype(BF16)
    ik = misc_ref[...][:, :IDX_DIM]
    kidx_ref[...] = _layernorm(ik, ig_ref[...], ib_ref[...]).astype(BF16)


def _sa_prep(proj, q_norm, w_uq, w_uk, w_qidx, kv_norm, ig, ib):
    n = proj.shape[0]
    tm = ROW_TILE
    const2 = lambda shp: pl.BlockSpec(shp, lambda i: (0, 0))
    return pl.pallas_call(
        _sa_prep_kernel,
        grid=(n // tm,),
        in_specs=[pl.BlockSpec((tm, SA_Q_RANK), lambda i: (i, COL_CQ // SA_Q_RANK)),
                  pl.BlockSpec((tm, SA_KV_RANK), lambda i: (i, COL_CKV // SA_KV_RANK)),
                  pl.BlockSpec((tm, LANE), lambda i: (i, COL_MISC // LANE)),
                  const2((1, SA_Q_RANK)),
                  const2((SA_Q_RANK, SA_HEADS * SA_DK)),
                  pl.BlockSpec((SA_HEADS, SA_DK, SA_KV_RANK), lambda i: (0, 0, 0)),
                  const2((SA_Q_RANK, IDX_HEADS * IDX_DIM)),
                  const2((1, SA_KV_RANK)),
                  const2((1, IDX_DIM)), const2((1, IDX_DIM))],
        out_specs=[pl.BlockSpec((tm, SA_HEADS * SA_KV_RANK), lambda i: (i, 0)),
                   pl.BlockSpec((IDX_HEADS, tm, IDX_DIM), lambda i: (0, i, 0)),
                   pl.BlockSpec((tm, SA_KV_RANK), lambda i: (i, 0)),
                   pl.BlockSpec((tm, IDX_DIM), lambda i: (i, 0))],
        out_shape=[jax.ShapeDtypeStruct((n, SA_HEADS * SA_KV_RANK), BF16),
                   jax.ShapeDtypeStruct((IDX_HEADS, n, IDX_DIM), BF16),
                   jax.ShapeDtypeStruct((n, SA_KV_RANK), BF16),
                   jax.ShapeDtypeStruct((n, IDX_DIM), BF16)],
        compiler_params=_params(("parallel",)),
        name="sa_prep",
    )(proj, proj, proj, q_norm, w_uq, w_uk, w_qidx, kv_norm, ig, ib)


def _sa_body(j, qidx_ref, qlat_ref, misc_ref, kidx_ref, ckv_ref, o_ref, key_s, tie_s, bound_s,
             *, k_top, t_eff):
    qb = qlat_ref.shape[0]
    kidx = kidx_ref[0:t_eff, :]
    w_rows = misc_ref[...].T * ((IDX_HEADS ** -0.5) * (IDX_DIM ** -0.5))
    score = jnp.zeros((t_eff, qb), F32)
    for h in range(IDX_HEADS):
        logit = jnp.maximum(_dot_nt(kidx, qidx_ref[h]), 0.0)
        score = score + w_rows[MISC_IW + h:MISC_IW + h + 1, :] * logit

    key_pos = lax.broadcasted_iota(I32, (t_eff, qb), 0)
    t_pos = j * qb + lax.broadcasted_iota(I32, (t_eff, qb), 1)
    causal = key_pos <= t_pos
    bits = pltpu.bitcast(score, I32)
    key = bits ^ ((bits >> 31) & 0x7FFFFFFF)
    key_s[0:t_eff, :] = jnp.where(causal, key, INT_MIN)

    def count(flags):
        ways = 8
        accs = [flags[g * 8:(g + 1) * 8, :] for g in range(ways)]
        for r in range(ways, t_eff // 8):
            accs[r % ways] = accs[r % ways] + flags[r * 8:(r + 1) * 8, :]
        while len(accs) > 1:
            accs = [a + b for a, b in zip(accs[0::2], accs[1::2])]
        return jnp.sum(accs[0], axis=0, keepdims=True)

    def thr_step(i, thr_u):
        cand_u = thr_u | lax.shift_left(jnp.int32(1), 31 - i)
        cnt = count(jnp.where(key_s[0:t_eff, :] >= (cand_u ^ INT_MIN), 1.0, 0.0))
        return jnp.where(cnt >= k_top, cand_u, thr_u)

    thr = lax.fori_loop(0, 32, thr_step, jnp.zeros((1, qb), I32), unroll=2) ^ INT_MIN
    key = key_s[0:t_eff, :]
    above = key > thr
    tie = jnp.where(causal, jnp.where(key == thr, 1.0, 0.0), 0.0).astype(F32)
    tie_s[0:t_eff, :] = tie
    need = k_top - count(jnp.where(above, 1.0, 0.0))
    excess = count(tie) - need

    n_bits = t_eff.bit_length()
    bound_s[...] = jnp.full((1, qb), 2 ** n_bits - 1, I32)

    @pl.when(jnp.max(excess) > 0.0)
    def _():
        def tie_step(i, bound):
            cand = bound | lax.shift_left(jnp.int32(1), (n_bits - 1) - i)
            cnt = count(jnp.where(key_pos < cand, tie_s[0:t_eff, :], 0.0))
            return jnp.where(cnt <= need, cand, bound)

        bound_s[...] = lax.fori_loop(0, n_bits, tie_step, jnp.zeros((1, qb), I32))

    chosen = jnp.where(above, 1.0, jnp.where(key_pos < bound_s[...], tie_s[0:t_eff, :], 0.0))
    bias = jnp.where(chosen > 0.0, 0.0, -jnp.inf).astype(F32).T

    ckv = ckv_ref[0:t_eff, :]
    for h in range(SA_HEADS):
        s = _dot_nt(qlat_ref[:, h * SA_KV_RANK:(h + 1) * SA_KV_RANK], ckv) + bias
        m = jnp.max(s, axis=1, keepdims=True)
        p = jnp.exp(s - m)
        l = jnp.sum(p, axis=1, keepdims=True)
        o = _dot(p.astype(BF16), ckv) / l
        o_ref[:, h * SA_KV_RANK:(h + 1) * SA_KV_RANK] = o.astype(BF16)


def _sa_kernel(qidx_ref, qlat_ref, misc_ref, kidx_ref, ckv_ref, o_ref, key_s, tie_s, bound_s,
               *, k_top, n_cls):
    qb = qlat_ref.shape[0]
    t = kidx_ref.shape[0]
    j = pl.program_id(1)
    per = (t // qb) // n_cls
    for cls in range(n_cls):
        body = functools.partial(_sa_body, j, qidx_ref, qlat_ref, misc_ref, kidx_ref, ckv_ref,
                                 o_ref, key_s, tie_s, bound_s, k_top=k_top,
                                 t_eff=(cls + 1) * per * qb)
        pl.when(j // per == cls)(body)


def _sparse_attention(qidx, qlat, proj, kidx, ckvn, batch, t):
    n = proj.shape[0]
    nq = t // Q_TILE
    k_top = min(TOPK_MAX, t // 4)
    n_cls = SA_CLASSES if nq % SA_CLASSES == 0 else 1
    return pl.pallas_call(
        functools.partial(_sa_kernel, k_top=k_top, n_cls=n_cls),
        grid=(batch, nq),
        in_specs=[pl.BlockSpec((IDX_HEADS, Q_TILE, IDX_DIM), lambda b, j: (0, b * nq + j, 0)),
                  pl.BlockSpec((Q_TILE, SA_HEADS * SA_KV_RANK), lambda b, j: (b * nq + j, 0)),
                  pl.BlockSpec((Q_TILE, LANE), lambda b, j: (b * nq + j, COL_MISC // LANE)),
                  pl.BlockSpec((t, IDX_DIM), lambda b, j: (b, 0)),
                  pl.BlockSpec((t, SA_KV_RANK), lambda b, j: (b, 0))],
        out_specs=pl.BlockSpec((Q_TILE, SA_HEADS * SA_KV_RANK), lambda b, j: (b * nq + j, 0)),
        out_shape=jax.ShapeDtypeStruct((n, SA_HEADS * SA_KV_RANK), BF16),
        scratch_shapes=[pltpu.VMEM((t, Q_TILE), I32), pltpu.VMEM((t, Q_TILE), F32),
                        pltpu.VMEM((1, Q_TILE), I32)],
        compiler_params=_params(("parallel", "parallel")),
        name="sparse_attention",
    )(qidx, qlat, proj, kidx, ckvn)


def _outproj_kernel(ydn_ref, olat_ref, x_ref, wdn_ref, wsa_ref, wuv_ref, g_ref, b_ref, rw_ref,
                    rb_ref, x1_ref, x1b_ref, lg_ref):
    ysa = _dot(olat_ref[...], wuv_ref[...])
    mix = _dot(ydn_ref[...].astype(BF16), wdn_ref[...]) + _dot(ysa.astype(BF16), wsa_ref[...])
    x1 = _layernorm(DEEPNORM_ALPHA * x_ref[...] + mix, g_ref[...], b_ref[...])
    x1_ref[...] = x1
    x1b_ref[...] = x1.astype(BF16)
    lg_ref[...] = _dot(x1, rw_ref[...], precision=_HI) + rb_ref[...]


def _outproj(ydn, olat, x2, w_o_dn, w_o_sa, wuv_bd, g, b, rw, rb):
    n = x2.shape[0]
    tm = ROW_TILE
    row = lambda w: pl.BlockSpec((tm, w), lambda i: (i, 0))
    const2 = lambda shp: pl.BlockSpec(shp, lambda i: (0, 0))
    return pl.pallas_call(
        _outproj_kernel,
        grid=(n // tm,),
        in_specs=[row(DN_WIDTH), row(SA_HEADS * SA_KV_RANK), row(D_MODEL),
                  const2((DN_WIDTH, D_MODEL)), const2((SA_WIDTH, D_MODEL)),
                  const2((SA_HEADS * SA_KV_RANK, SA_WIDTH)),
                  const2((1, D_MODEL)), const2((1, D_MODEL)),
                  const2((D_MODEL, N_EXPERTS)), const2((1, N_EXPERTS))],
        out_specs=[row(D_MODEL), row(D_MODEL), row(N_EXPERTS)],
        out_shape=[jax.ShapeDtypeStruct((n, D_MODEL), F32),
                   jax.ShapeDtypeStruct((n, D_MODEL), BF16),
                   jax.ShapeDtypeStruct((n, N_EXPERTS), F32)],
        compiler_params=_params(("parallel",)),
        name="outproj",
    )(ydn, olat, x2, w_o_dn, w_o_sa, wuv_bd, g, b, rw, rb)


def _route_kernel(lg_ref, rt_ref, gt_ref, cnt_ref, pad_s):
    s_tok = lg_ref.shape[0]
    l = lg_ref[...]
    lane = lax.broadcasted_iota(I32, l.shape, 1)
    vals, hots = [], []
    for _ in range(TOP_K):
        m = jnp.max(l, axis=1, keepdims=True)
        idx = jnp.min(jnp.where(l == m, lane, N_EXPERTS), axis=1, keepdims=True)
        hot = lane == idx
        vals.append(m)
        hots.append(hot)
        l = jnp.where(hot, -jnp.inf, l)
    exps = [jnp.exp(v - vals[0]) for v in vals]
    den = exps[0]
    for e in exps[1:]:
        den = den + e
    gates = jnp.zeros(l.shape, F32)
    chosen = jnp.zeros(l.shape, F32)
    for hot, e in zip(hots, exps):
        gates = gates + jnp.where(hot, e / den, 0.0)
        chosen = chosen + jnp.where(hot, 1.0, 0.0)

    blk = 256
    ii = lax.broadcasted_iota(I32, (blk, blk), 0)
    jj = lax.broadcasted_iota(I32, (blk, blk), 1)
    tri = jnp.where(ii > jj, 1.0, 0.0).astype(BF16)
    carry = jnp.zeros((1, N_EXPERTS), F32)
    for ci in range(s_tok // blk):
        mc = chosen[ci * blk:(ci + 1) * blk, :]
        rk = _dot(tri, mc.astype(BF16)) + carry
        rank = jnp.where(mc > 0.0, rk, -1.0)
        pad_s[ci * blk:(ci + 1) * blk, 0:N_EXPERTS] = rank
        pad_s[ci * blk:(ci + 1) * blk, N_EXPERTS:2 * N_EXPERTS] = gates[ci * blk:(ci + 1) * blk, :]
        carry = carry + jnp.sum(mc, axis=0, keepdims=True)
    cnt_ref[...] = carry.astype(I32)
    pad_s[:, 2 * N_EXPERTS:] = jnp.zeros((s_tok, LANE - 2 * N_EXPERTS), F32)
    tr = pad_s[...].T
    rt_ref[...] = tr[0:N_EXPERTS, :].astype(I32)
    gt_ref[...] = tr[N_EXPERTS:2 * N_EXPERTS, :]


def _route(logits, s_tok):
    n = logits.shape[0]
    n_super = n // s_tok
    return pl.pallas_call(
        _route_kernel,
        grid=(n_super,),
        in_specs=[pl.BlockSpec((s_tok, N_EXPERTS), lambda b: (b, 0))],
        out_specs=[pl.BlockSpec((None, N_EXPERTS, s_tok), lambda b: (b, 0, 0)),
                   pl.BlockSpec((None, N_EXPERTS, s_tok), lambda b: (b, 0, 0)),
                   pl.BlockSpec((None, 1, N_EXPERTS), lambda b: (b, 0, 0))],
        out_shape=[jax.ShapeDtypeStruct((n_super, N_EXPERTS, s_tok), I32),
                   jax.ShapeDtypeStruct((n_super, N_EXPERTS, s_tok), F32),
                   jax.ShapeDtypeStruct((n_super, 1, N_EXPERTS), I32)],
        scratch_shapes=[pltpu.VMEM((s_tok, LANE), F32)],
        compiler_params=_params(("parallel",)),
        name="route",
    )(logits)


def _moe_kernel(cnt_ref, x_ref, rt_ref, gt_ref, wgu_ref, bgu_ref, wdn_ref, bdn_ref, o_ref):
    b = pl.program_id(0)
    e = pl.program_id(1)
    s_tok = x_ref.shape[0]
    m = MOE_ROWS

    @pl.when(e == 0)
    def _():
        o_ref[...] = jnp.zeros(o_ref.shape, F32)

    cnt = cnt_ref[b * N_EXPERTS + e]
    n_pass = (cnt + m - 1) // m
    rank_row = rt_ref[pl.ds(e, 1), :]
    gate_row = gt_ref[pl.ds(e, 1), :]
    sub_iota = lax.broadcasted_iota(I32, (MXU_DIM, s_tok), 0)
    pad_rows = jnp.zeros((MXU_DIM - m, D_MODEL), BF16)

    def one_pass(p, carry):
        hit = (rank_row - p * m) == sub_iota
        pick = jnp.where(hit, 1.0, 0.0).astype(BF16)
        xs = _dot(pick[0:m, :], x_ref[...]).astype(BF16)
        hgu = _dot(xs, wgu_ref[...]) + bgu_ref[...]
        gate = jnp.minimum(hgu[:, :D_FF], SWIGLU_LIMIT)
        up = jnp.clip(hgu[:, D_FF:], -SWIGLU_LIMIT, SWIGLU_LIMIT)
        hdn = (up + 1.0) * (gate * _sigmoid(SWIGLU_ALPHA * gate))
        y = _dot(hdn.astype(BF16), wdn_ref[...]) + bdn_ref[...]
        g_rows = jnp.sum(jnp.where(hit[0:m, :], gate_row, 0.0), axis=1, keepdims=True)
        ys = jnp.concatenate([(y * g_rows).astype(BF16), pad_rows], axis=0)
        o_ref[...] += _dot_tn(pick, ys)
        return carry

    lax.fori_loop(0, n_pass, one_pass, 0)


def _moe(counts, x1b, rt, gt, wgu, bgu, wdn, bdn, s_tok):
    n = x1b.shape[0]
    return pl.pallas_call(
        _moe_kernel,
        grid_spec=pltpu.PrefetchScalarGridSpec(
            num_scalar_prefetch=1,
            grid=(n // s_tok, N_EXPERTS),
            in_specs=[pl.BlockSpec((s_tok, D_MODEL), lambda b, e, c: (b, 0)),
                      pl.BlockSpec((None, N_EXPERTS, s_tok), lambda b, e, c: (b, 0, 0)),
                      pl.BlockSpec((None, N_EXPERTS, s_tok), lambda b, e, c: (b, 0, 0)),
                      pl.BlockSpec((None, D_MODEL, 2 * D_FF), lambda b, e, c: (e, 0, 0)),
                      pl.BlockSpec((None, 1, 2 * D_FF), lambda b, e, c: (e, 0, 0)),
                      pl.BlockSpec((None, D_FF, D_MODEL), lambda b, e, c: (e, 0, 0)),
                      pl.BlockSpec((None, 1, D_MODEL), lambda b, e, c: (e, 0, 0))],
            out_specs=pl.BlockSpec((s_tok, D_MODEL), lambda b, e, c: (b, 0))),
        out_shape=jax.ShapeDtypeStruct((n, D_MODEL), F32),
        compiler_params=_params(("parallel", "arbitrary")),
        name="moe",
    )(counts, x1b, rt, gt, wgu, bgu, wdn, bdn)


def _ln2_kernel(x_ref, f_ref, g_ref, b_ref, o_ref):
    o_ref[...] = _layernorm(DEEPNORM_ALPHA * x_ref[...] + f_ref[...], g_ref[...], b_ref[...])


def _ln2(x1, ffn, g, b):
    n = x1.shape[0]
    tm = ROW_TILE
    row = pl.BlockSpec((tm, D_MODEL), lambda i: (i, 0))
    const = pl.BlockSpec((1, D_MODEL), lambda i: (0, 0))
    return pl.pallas_call(
        _ln2_kernel,
        grid=(n // tm,),
        in_specs=[row, row, const, const],
        out_specs=row,
        out_shape=jax.ShapeDtypeStruct((n, D_MODEL), F32),
        compiler_params=_params(("parallel",)),
        name="ln2",
    )(x1, ffn, g, b)


def _permute_w_in(w_in):
    depth = w_in.shape[0]
    o_db = 4 * DN_QK
    o_da = o_db + DN_HEADS
    o_cq = o_da + DN_HEADS
    o_ckv = o_cq + SA_Q_RANK
    o_ik = o_ckv + SA_KV_RANK
    o_iw = o_ik + IDX_DIM
    o_end = o_iw + IDX_HEADS
    parts = [w_in[..., :o_db], w_in[..., o_cq:o_ckv], w_in[..., o_ckv:o_ik], w_in[..., o_ik:o_iw],
             w_in[..., o_iw:o_end], w_in[..., o_db:o_da], w_in[..., o_da:o_cq]]
    width = sum(p.shape[-1] for p in parts)
    parts.append(jnp.zeros((depth, D_MODEL, PROJ_WIDTH - width), w_in.dtype))
    return jnp.concatenate(parts, axis=-1).astype(BF16)


def kernel(x, w_in, dn_conv, dn_a_log, dn_dt_bias, dn_norm, sa_q_norm, sa_w_uq, sa_kv_norm, sa_w_uk, sa_w_uv, idx_w_q, idx_k_norm_g, idx_k_norm_b, w_o, ln1_g, ln1_b, router_w, router_b, w_gate_up, b_gate_up, w_down, b_down, ln2_g, ln2_b):
    batch, t, d = x.shape
    depth = w_in.shape[0]
    n = batch * t
    nc = t // DN_CHUNK
    s_moe = min(MOE_TOKENS, n)

    w_in_p = _permute_w_in(w_in)
    w_uq = sa_w_uq.astype(BF16)
    w_uk = sa_w_uk.astype(BF16)
    w_qidx = idx_w_q.astype(BF16)
    w_o_b = w_o.astype(BF16)
    eye_h = jnp.eye(SA_HEADS, dtype=F32)
    wuv_bd = jnp.einsum('lhrv,hg->lhrgv', sa_w_uv, eye_h).reshape(
        depth, SA_HEADS * SA_KV_RANK, SA_WIDTH).astype(BF16)
    wgu = w_gate_up.astype(BF16)
    wdn = w_down.astype(BF16)
    dn_sm = jnp.stack([dn_a_log, dn_dt_bias], axis=1)

    x2 = x.reshape(n, d)
    for l in range(depth):
        proj = _inproj(x2, w_in_p[l])
        gates = proj[:, COL_MISC + MISC_DB:COL_MISC + MISC_DA + DN_HEADS]
        gates = gates.reshape(batch, nc, DN_CHUNK, 2, DN_HEADS).transpose(0, 4, 3, 1, 2)
        ydn = _deltanet(proj, gates, dn_sm[l], dn_conv[l], dn_norm[l][None, :], batch, t)
        qlat, qidx, ckvn, kidx = _sa_prep(
            proj, sa_q_norm[l][None, :], w_uq[l], w_uk[l], w_qidx[l], sa_kv_norm[l][None, :],
            idx_k_norm_g[l][None, :], idx_k_norm_b[l][None, :])
        olat = _sparse_attention(qidx, qlat, proj, kidx, ckvn, batch, t)
        x1, x1b, logits = _outproj(
            ydn, olat, x2, w_o_b[l, :DN_WIDTH], w_o_b[l, DN_WIDTH:], wuv_bd[l],
            ln1_g[l][None, :], ln1_b[l][None, :], router_w[l], router_b[l][None, :])
        rt, gt, cnt = _route(logits, s_moe)
        ffn = _moe(cnt.reshape(-1), x1b, rt, gt, wgu[l], b_gate_up[l][:, None, :], wdn[l],
                   b_down[l][:, None, :], s_moe)
        x2 = _ln2(x1, ffn, ln2_g[l][None, :], ln2_b[l][None, :])
    return x2.reshape(batch, t, d)
```

```python
import functools

import jax
import jax.numpy as jnp
from jax import lax
from jax.experimental import pallas as pl
from jax.experimental.pallas import tpu as pltpu

F32 = jnp.float32
BF16 = jnp.bfloat16
I32 = jnp.int32

D_MODEL = 1024
DN_HEADS = 4
DN_DK = 128
DN_CONV = 4
DN_QK = DN_HEADS * DN_DK
DN_WIDTH = DN_HEADS * DN_DK
SA_HEADS = 8
SA_DK = 64
SA_DV = 64
SA_Q_RANK = 256
SA_KV_RANK = 128
SA_WIDTH = SA_HEADS * SA_DV
IDX_HEADS = 8
IDX_DIM = 64
TOPK_MAX = 256
N_EXPERTS = 32
TOP_K = 4
D_FF = 1024
SWIGLU_LIMIT = 7.0
SWIGLU_ALPHA = 1.702
EPS = 1e-6
DEPTH = 4
DEEPNORM_ALPHA = (2 * DEPTH) ** 0.25

LANE = 128
MXU_DIM = 256
DN_CHUNK = 128
DN_PAIR = 2
PROJ_WIDTH = 2560
COL_Z = 3 * DN_QK
COL_CQ = COL_Z + DN_WIDTH
COL_CKV = COL_CQ + SA_Q_RANK
COL_MISC = COL_CKV + SA_KV_RANK
MISC_IW = IDX_DIM
MISC_DB = MISC_IW + IDX_HEADS
MISC_DA = MISC_DB + DN_HEADS
ROW_TILE = 512
Q_TILE = 128
SA_CLASSES = 8
MOE_TOKENS = 1024
MOE_ROWS = 160
MOE_GROUP = 2
INT_MIN = -(2 ** 31)
LOG2E = 1.4426950408889634
VMEM_LIMIT = 56 * 1024 * 1024

_HI = lax.Precision.HIGHEST


def _dot(a, b, precision=None):
    return jnp.dot(a, b, preferred_element_type=F32, precision=precision)


def _dot_nt(a, b, precision=None):
    return lax.dot_general(a, b, (((1,), (1,)), ((), ())), preferred_element_type=F32,
                           precision=precision)


def _dot_tn(a, b):
    return lax.dot_general(a, b, (((0,), (0,)), ((), ())), preferred_element_type=F32)


def _split(x):
    hi = x.astype(BF16)
    return hi, (x - hi.astype(F32)).astype(BF16)


def _sigmoid(x):
    return 1.0 / (1.0 + jnp.exp(-x))


def _silu(x):
    return x * _sigmoid(x)


def _softplus(x):
    return jnp.maximum(x, 0.0) + jnp.log(1.0 + jnp.exp(-jnp.abs(x)))


def _layernorm(x, g, b):
    mu = jnp.mean(x, axis=-1, keepdims=True)
    xc = x - mu
    var = jnp.mean(xc * xc, axis=-1, keepdims=True)
    return xc * lax.rsqrt(var + EPS) * g + b


def _rmsnorm(x, g):
    return x * lax.rsqrt(jnp.mean(x * x, axis=-1, keepdims=True) + EPS) * g


def _params(sem):
    return pltpu.CompilerParams(dimension_semantics=sem, vmem_limit_bytes=VMEM_LIMIT)


def _inproj_kernel(x_ref, w_ref, o_ref):
    o_ref[...] = _dot(x_ref[...].astype(BF16), w_ref[...])


def _inproj(layer, x2, w):
    n = x2.shape[0]
    return pl.pallas_call(
        _inproj_kernel,
        grid=(n // ROW_TILE,),
        in_specs=[pl.BlockSpec((ROW_TILE, D_MODEL), lambda i: (i, 0)),
                  pl.BlockSpec((None, D_MODEL, PROJ_WIDTH), lambda i: (layer, 0, 0))],
        out_specs=pl.BlockSpec((ROW_TILE, PROJ_WIDTH), lambda i: (i, 0)),
        out_shape=jax.ShapeDtypeStruct((n, PROJ_WIDTH), F32),
        compiler_params=_params(("parallel",)),
        name="inproj",
    )(x2, w)


def _dn_kernel(sm_ref, q_ref, k_ref, v_ref, z_ref, g_ref, wq_ref, wk_ref, wv_ref, nw_ref,
               o_ref, beta_s, gcum_s, u_s, w_s, at_s, qd_s, kdt_s, gl_s):
    hp = pl.program_id(1)
    t = q_ref.shape[0]
    c = DN_CHUNK
    n_chunks = t // c
    ii = lax.broadcasted_iota(I32, (c, c), 0)
    jj = lax.broadcasted_iota(I32, (c, c), 1)
    lower_incl = ii >= jj
    cum_mat = jnp.where(ii <= jj, 1.0, 0.0).astype(F32)
    nw = nw_ref[...]

    def conv_silu(x_ref, w_ref, ci, r0, sl):
        w = w_ref[:, sl]
        cur = x_ref[pl.ds(r0, c), sl]
        before = x_ref[pl.ds(pl.multiple_of(jnp.maximum(r0 - 8, 0), 8), 8), sl]
        both = jnp.concatenate([jnp.where(ci > 0, before, 0.0), cur], axis=0)
        y = cur * w[DN_CONV - 1:DN_CONV, :]
        for s in range(1, DN_CONV):
            y = y + both[8 - s:8 - s + c, :] * w[DN_CONV - 1 - s:DN_CONV - s, :]
        return _silu(y)

    def l2norm(x):
        return x * lax.rsqrt(jnp.sum(x * x, axis=-1, keepdims=True) + EPS)

    for hh in range(DN_PAIR):
        sl = slice(hh * LANE, (hh + 1) * LANE)
        head = hp * DN_PAIR + hh
        a_coef = jnp.exp(jnp.full((1, LANE), sm_ref[0, head], F32))
        beta_s[hh] = _sigmoid(g_ref[hh, 0])
        g_log = -a_coef * _softplus(g_ref[hh, 1] + sm_ref[1, head])
        gcum_s[hh] = _dot(g_log, cum_mat, precision=_HI)

    def local(ci, carry):
        r0 = pl.multiple_of(ci * c, c)
        a_pows, sols = [], []
        for hh in range(DN_PAIR):
            g_row = jnp.broadcast_to(gcum_s[hh, pl.ds(ci, 1), :], (c, c))
            g_col = g_row.T
            b_col = jnp.broadcast_to(beta_s[hh, pl.ds(ci, 1), :], (c, c)).T
            g_last = jnp.broadcast_to(g_row[:, c - 1:c], (c, c))
            decay = jnp.exp(jnp.where(lower_incl, g_col - g_row, -jnp.inf))
            sl = slice(hh * LANE, (hh + 1) * LANE)
            qc = l2norm(conv_silu(q_ref, wq_ref, ci, r0, sl)) * (DN_DK ** -0.5)
            kc = l2norm(conv_silu(k_ref, wk_ref, ci, r0, sl))
            vc = conv_silu(v_ref, wv_ref, ci, r0, sl)
            kb = kc * b_col
            kcb = kc.astype(BF16)
            a_pows.append(jnp.where(ii > jj, -(_dot_nt(kb.astype(BF16), kcb) * decay), 0.0))
            sols.append(jnp.concatenate([vc * b_col, kb * jnp.exp(g_col)], axis=1))
            at_s[hh, pl.ds(r0, c), :] = (_dot_nt(qc.astype(BF16), kcb) * decay).astype(BF16)
            kdt_s[hh, pl.ds(r0, c), :] = (kc * jnp.exp(g_last - g_col)).T.astype(BF16)
            qd_s[hh, pl.ds(r0, c), :] = (qc * jnp.exp(g_col)).astype(BF16)
            gl_s[hh, pl.ds(ci, 1), :] = jnp.exp(g_last[0:1, :])

        n_steps = (c - 1).bit_length()
        for step in range(n_steps):
            last = step + 1 == n_steps
            rhs = jnp.concatenate(
                [sols[hh] if last else jnp.concatenate([sols[hh], a_pows[hh]], axis=1)
                 for hh in range(DN_PAIR)], axis=0)
            a_bd = jnp.concatenate(
                [jnp.concatenate([a_pows[hh] if g == hh else jnp.zeros((c, c), F32)
                                  for g in range(DN_PAIR)], axis=1)
                 for hh in range(DN_PAIR)], axis=0)
            r_hi, r_lo = _split(rhs)
            both = _dot(a_bd.astype(BF16), jnp.concatenate([r_hi, r_lo], axis=1))
            prod = both[:, 0:rhs.shape[1]] + both[:, rhs.shape[1]:]
            for hh in range(DN_PAIR):
                mine = prod[hh * c:(hh + 1) * c, :]
                sols[hh] = sols[hh] + mine[:, 0:2 * c]
                if not last:
                    a_pows[hh] = mine[:, 2 * c:]
        for hh in range(DN_PAIR):
            u_s[hh, pl.ds(r0, c), :] = sols[hh][:, :c]
            w_s[hh, pl.ds(r0, c), :] = sols[hh][:, c:].astype(BF16)
        return carry

    lax.fori_loop(0, n_chunks, local, 0, unroll=2)

    def recur(ci, states):
        r0 = pl.multiple_of(ci * c, c)
        new_states = []
        for hh in range(DN_PAIR):
            sl = slice(hh * LANE, (hh + 1) * LANE)
            state = states[hh]
            sb = state.astype(BF16)
            v_new = u_s[hh, pl.ds(r0, c), :] - _dot(w_s[hh, pl.ds(r0, c), :], sb)
            vb = v_new.astype(BF16)
            o = _dot(qd_s[hh, pl.ds(r0, c), :], sb) + _dot(at_s[hh, pl.ds(r0, c), :], vb)
            g_last = jnp.broadcast_to(gl_s[hh, pl.ds(ci, 1), :], (c, c))
            new_states.append(state * g_last + _dot(kdt_s[hh, pl.ds(r0, c), :], vb))
            o_ref[pl.ds(r0, c), sl] = _rmsnorm(o, nw) * _silu(z_ref[pl.ds(r0, c), sl])
        return tuple(new_states)

    lax.fori_loop(0, n_chunks, recur, tuple(jnp.zeros((c, c), F32) for _ in range(DN_PAIR)))


def _deltanet(proj, gates, dn_sm, conv_w, norm_w, batch, t):
    n = proj.shape[0]
    nc = t // DN_CHUNK
    width = DN_PAIR * LANE
    n_pairs = DN_HEADS // DN_PAIR
    blk = lambda off: pl.BlockSpec((t, width), lambda b, h: (b, off + h))
    wblk = lambda off: pl.BlockSpec((DN_CONV, width), lambda b, h: (0, off + h))
    seq_f32 = pltpu.VMEM((DN_PAIR, t, LANE), F32)
    seq_bf16 = pltpu.VMEM((DN_PAIR, t, LANE), BF16)
    per_chunk = pltpu.VMEM((DN_PAIR, nc, LANE), F32)
    return pl.pallas_call(
        _dn_kernel,
        grid=(batch, n_pairs),
        in_specs=[pl.BlockSpec(memory_space=pltpu.SMEM),
                  blk(0), blk(n_pairs), blk(2 * n_pairs), blk(3 * n_pairs),
                  pl.BlockSpec((None, DN_PAIR, 2, nc, LANE), lambda b, h: (b, h, 0, 0, 0)),
                  wblk(0), wblk(n_pairs), wblk(2 * n_pairs),
                  pl.BlockSpec((1, LANE), lambda b, h: (0, 0))],
        out_specs=pl.BlockSpec((t, width), lambda b, h: (b, h)),
        out_shape=jax.ShapeDtypeStruct((n, DN_WIDTH), F32),
        scratch_shapes=[per_chunk, per_chunk,
                        seq_f32, seq_bf16, seq_bf16, seq_bf16, seq_bf16, per_chunk],
        compiler_params=_params(("parallel", "parallel")),
        name="deltanet",
    )(dn_sm, proj, proj, proj, proj, gates, conv_w, conv_w, conv_w, norm_w)


def _sa_prep_kernel(cq_ref, ckv_ref, misc_ref, qn_ref, wuq_ref, wuk_ref, wqi_ref, kvn_ref,
                    ig_ref, ib_ref, qlat_ref, qidx_ref, ckvn_ref, kidx_ref):
    cq = _rmsnorm(cq_ref[...], qn_ref[...]).astype(BF16)
    q = _dot(cq, wuq_ref[...])
    qi = _dot(cq, wqi_ref[...])
    for h in range(SA_HEADS):
        qh = q[:, h * SA_DK:(h + 1) * SA_DK].astype(BF16)
        ql = _dot(qh, wuk_ref[h]) * ((SA_DK ** -0.5) * LOG2E)
        qlat_ref[:, h * SA_KV_RANK:(h + 1) * SA_KV_RANK] = ql.astype(BF16)
    for h in range(IDX_HEADS):
        qidx_ref[h] = qi[:, h * IDX_DIM:(h + 1) * IDX_DIM].astype(BF16)
    ckvn_ref[...] = _rmsnorm(ckv_ref[...], kvn_ref[...]).astype(BF16)
    ik = misc_ref[...][:, :IDX_DIM]
    kidx_ref[...] = _layernorm(ik, ig_ref[...], ib_ref[...]).astype(BF16)


def _sa_prep(proj, q_norm, w_uq, w_uk, w_qidx, kv_norm, ig, ib):
    n = proj.shape[0]
    tm = ROW_TILE
    const2 = lambda shp: pl.BlockSpec(shp, lambda i: (0, 0))
    return pl.pallas_call(
        _sa_prep_kernel,
        grid=(n // tm,),
        in_specs=[pl.BlockSpec((tm, SA_Q_RANK), lambda i: (i, COL_CQ // SA_Q_RANK)),
                  pl.BlockSpec((tm, SA_KV_RANK), lambda i: (i, COL_CKV // SA_KV_RANK)),
                  pl.BlockSpec((tm, LANE), lambda i: (i, COL_MISC // LANE)),
                  const2((1, SA_Q_RANK)),
                  const2((SA_Q_RANK, SA_HEADS * SA_DK)),
                  pl.BlockSpec((SA_HEADS, SA_DK, SA_KV_RANK), lambda i: (0, 0, 0)),
                  const2((SA_Q_RANK, IDX_HEADS * IDX_DIM)),
                  const2((1, SA_KV_RANK)),
                  const2((1, IDX_DIM)), const2((1, IDX_DIM))],
        out_specs=[pl.BlockSpec((tm, SA_HEADS * SA_KV_RANK), lambda i: (i, 0)),
                   pl.BlockSpec((IDX_HEADS, tm, IDX_DIM), lambda i: (0, i, 0)),
                   pl.BlockSpec((tm, SA_KV_RANK), lambda i: (i, 0)),
                   pl.BlockSpec((tm, IDX_DIM), lambda i: (i, 0))],
        out_shape=[jax.ShapeDtypeStruct((n, SA_HEADS * SA_KV_RANK), BF16),
                   jax.ShapeDtypeStruct((IDX_HEADS, n, IDX_DIM), BF16),
                   jax.ShapeDtypeStruct((n, SA_KV_RANK), BF16),
                   jax.ShapeDtypeStruct((n, IDX_DIM), BF16)],
        compiler_params=_params(("parallel",)),
        name="sa_prep",
    )(proj, proj, proj, q_norm, w_uq, w_uk, w_qidx, kv_norm, ig, ib)


def _sa_body(j, qidx_ref, qlat_ref, misc_ref, kidx_ref, ckv_ref, o_ref, key_s, tie_s, bound_s,
             *, k_top, t_eff):
    qb = qlat_ref.shape[0]
    kidx = kidx_ref[0:t_eff, :]
    w_rows = misc_ref[...].T * ((IDX_HEADS ** -0.5) * (IDX_DIM ** -0.5))
    logits = jnp.maximum(_dot_nt(kidx, qidx_ref[...].reshape(IDX_HEADS * qb, IDX_DIM)), 0.0)
    score = jnp.zeros((t_eff, qb), F32)
    for h in range(IDX_HEADS):
        score = score + w_rows[MISC_IW + h:MISC_IW + h + 1, :] * logits[:, h * qb:(h + 1) * qb]

    key_pos = lax.broadcasted_iota(I32, (t_eff, qb), 0)
    t_pos = j * qb + lax.broadcasted_iota(I32, (t_eff, qb), 1)
    causal = key_pos <= t_pos
    bits = pltpu.bitcast(score, I32)
    key = bits ^ ((bits >> 31) & 0x7FFFFFFF)
    key_s[0:t_eff, :] = jnp.where(causal, key, INT_MIN)

    def count(flags):
        ways = 8
        accs = [flags[g * 8:(g + 1) * 8, :] for g in range(ways)]
        for r in range(ways, t_eff // 8):
            accs[r % ways] = accs[r % ways] + flags[r * 8:(r + 1) * 8, :]
        while len(accs) > 1:
            accs = [a + b for a, b in zip(accs[0::2], accs[1::2])]
        return jnp.sum(accs[0], axis=0, keepdims=True)

    def thr_step(i, thr_u):
        cand_u = thr_u | lax.shift_left(jnp.int32(1), 31 - i)
        cnt = count(jnp.where(key_s[0:t_eff, :] >= (cand_u ^ INT_MIN), 1.0, 0.0))
        return jnp.where(cnt >= k_top, cand_u, thr_u)

    thr = lax.fori_loop(0, 32, thr_step, jnp.zeros((1, qb), I32), unroll=2) ^ INT_MIN
    key = key_s[0:t_eff, :]
    above = key > thr
    tie = jnp.where(causal, jnp.where(key == thr, 1.0, 0.0), 0.0).astype(F32)
    tie_s[0:t_eff, :] = tie
    need = k_top - count(jnp.where(above, 1.0, 0.0))
    excess = count(tie) - need

    n_bits = t_eff.bit_length()
    bound_s[...] = jnp.full((1, qb), 2 ** n_bits - 1, I32)

    @pl.when(jnp.max(excess) > 0.0)
    def _():
        def tie_step(i, bound):
            cand = bound | lax.shift_left(jnp.int32(1), (n_bits - 1) - i)
            cnt = count(jnp.where(key_pos < cand, tie_s[0:t_eff, :], 0.0))
            return jnp.where(cnt <= need, cand, bound)

        bound_s[...] = lax.fori_loop(0, n_bits, tie_step, jnp.zeros((1, qb), I32))

    chosen = jnp.where(above, 1.0, jnp.where(key_pos < bound_s[...], tie_s[0:t_eff, :], 0.0))
    bias = jnp.where(chosen > 0.0, 0.0, -jnp.inf).astype(F32).T

    ckv = ckv_ref[0:t_eff, :]
    for h in range(SA_HEADS):
        s = _dot_nt(qlat_ref[:, h * SA_KV_RANK:(h + 1) * SA_KV_RANK], ckv) + bias
        m = jnp.max(s, axis=1, keepdims=True)
        p = jnp.exp2(s - m)
        l = jnp.sum(p, axis=1, keepdims=True)
        o = _dot(p.astype(BF16), ckv) / l
        o_ref[:, h * SA_KV_RANK:(h + 1) * SA_KV_RANK] = o.astype(BF16)


def _sa_kernel(qidx_ref, qlat_ref, misc_ref, kidx_ref, ckv_ref, o_ref, key_s, tie_s, bound_s,
               *, k_top, n_cls):
    qb = qlat_ref.shape[0]
    t = kidx_ref.shape[0]
    j = pl.program_id(1)
    per = (t // qb) // n_cls
    for cls in range(n_cls):
        body = functools.partial(_sa_body, j, qidx_ref, qlat_ref, misc_ref, kidx_ref, ckv_ref,
                                 o_ref, key_s, tie_s, bound_s, k_top=k_top,
                                 t_eff=(cls + 1) * per * qb)
        pl.when(j // per == cls)(body)


def _sparse_attention(qidx, qlat, proj, kidx, ckvn, batch, t):
    n = proj.shape[0]
    nq = t // Q_TILE
    k_top = min(TOPK_MAX, t // 4)
    n_cls = SA_CLASSES if nq % SA_CLASSES == 0 else 1
    return pl.pallas_call(
        functools.partial(_sa_kernel, k_top=k_top, n_cls=n_cls),
        grid=(batch, nq),
        in_specs=[pl.BlockSpec((IDX_HEADS, Q_TILE, IDX_DIM), lambda b, j: (0, b * nq + j, 0)),
                  pl.BlockSpec((Q_TILE, SA_HEADS * SA_KV_RANK), lambda b, j: (b * nq + j, 0)),
                  pl.BlockSpec((Q_TILE, LANE), lambda b, j: (b * nq + j, COL_MISC // LANE)),
                  pl.BlockSpec((t, IDX_DIM), lambda b, j: (b, 0)),
                  pl.BlockSpec((t, SA_KV_RANK), lambda b, j: (b, 0))],
        out_specs=pl.BlockSpec((Q_TILE, SA_HEADS * SA_KV_RANK), lambda b, j: (b * nq + j, 0)),
        out_shape=jax.ShapeDtypeStruct((n, SA_HEADS * SA_KV_RANK), BF16),
        scratch_shapes=[pltpu.VMEM((t, Q_TILE), I32), pltpu.VMEM((t, Q_TILE), F32),
                        pltpu.VMEM((1, Q_TILE), I32)],
        compiler_params=_params(("parallel", "parallel")),
        name="sparse_attention",
    )(qidx, qlat, proj, kidx, ckvn)


def _outproj_kernel(ydn_ref, olat_ref, x_ref, wdn_ref, wsa_ref, wuv_ref, g_ref, b_ref, rw_ref,
                    rb_ref, x1_ref, x1b_ref, lg_ref):
    ysa = _dot(olat_ref[...], wuv_ref[...])
    mix = _dot(ydn_ref[...].astype(BF16), wdn_ref[...]) + _dot(ysa.astype(BF16), wsa_ref[...])
    x1 = _layernorm(DEEPNORM_ALPHA * x_ref[...] + mix, g_ref[...], b_ref[...])
    x1_ref[...] = x1
    x_hi, x_lo = _split(x1)
    w_hi, w_lo = _split(rw_ref[...])
    x1b_ref[...] = x_hi
    lg_ref[...] = _dot(x_hi, w_hi) + (_dot(x_hi, w_lo) + _dot(x_lo, w_hi)) + rb_ref[...]


def _outproj(ydn, olat, x2, w_o_dn, w_o_sa, wuv_bd, g, b, rw, rb):
    n = x2.shape[0]
    tm = ROW_TILE
    row = lambda w: pl.BlockSpec((tm, w), lambda i: (i, 0))
    const2 = lambda shp: pl.BlockSpec(shp, lambda i: (0, 0))
    return pl.pallas_call(
        _outproj_kernel,
        grid=(n // tm,),
        in_specs=[row(DN_WIDTH), row(SA_HEADS * SA_KV_RANK), row(D_MODEL),
                  const2((DN_WIDTH, D_MODEL)), const2((SA_WIDTH, D_MODEL)),
                  const2((SA_HEADS * SA_KV_RANK, SA_WIDTH)),
                  const2((1, D_MODEL)), const2((1, D_MODEL)),
                  const2((D_MODEL, N_EXPERTS)), const2((1, N_EXPERTS))],
        out_specs=[row(D_MODEL), row(D_MODEL), row(N_EXPERTS)],
        out_shape=[jax.ShapeDtypeStruct((n, D_MODEL), F32),
                   jax.ShapeDtypeStruct((n, D_MODEL), BF16),
                   jax.ShapeDtypeStruct((n, N_EXPERTS), F32)],
        compiler_params=_params(("parallel",)),
        name="outproj",
    )(ydn, olat, x2, w_o_dn, w_o_sa, wuv_bd, g, b, rw, rb)


def _route_kernel(lg_ref, rt_ref, gt_ref, cnt_ref, pad_s):
    s_tok = lg_ref.shape[0]
    l = lg_ref[...]
    lane = lax.broadcasted_iota(I32, l.shape, 1)
    vals, hots = [], []
    for _ in range(TOP_K):
        m = jnp.max(l, axis=1, keepdims=True)
        idx = jnp.min(jnp.where(l == m, lane, N_EXPERTS), axis=1, keepdims=True)
        hot = lane == idx
        vals.append(m)
        hots.append(hot)
        l = jnp.where(hot, -jnp.inf, l)
    exps = [jnp.exp(v - vals[0]) for v in vals]
    den = exps[0]
    for e in exps[1:]:
        den = den + e
    gates = jnp.zeros(l.shape, F32)
    chosen = jnp.zeros(l.shape, F32)
    for hot, e in zip(hots, exps):
        gates = gates + jnp.where(hot, e / den, 0.0)
        chosen = chosen + jnp.where(hot, 1.0, 0.0)

    blk = 256
    ii = lax.broadcasted_iota(I32, (blk, blk), 0)
    jj = lax.broadcasted_iota(I32, (blk, blk), 1)
    tri = jnp.where(ii > jj, 1.0, 0.0).astype(BF16)
    carry = jnp.zeros((1, N_EXPERTS), F32)
    for ci in range(s_tok // blk):
        mc = chosen[ci * blk:(ci + 1) * blk, :]
        rk = _dot(tri, mc.astype(BF16)) + carry
        rank = jnp.where(mc > 0.0, rk, -1.0)
        pad_s[ci * blk:(ci + 1) * blk, 0:N_EXPERTS] = rank
        pad_s[ci * blk:(ci + 1) * blk, N_EXPERTS:2 * N_EXPERTS] = gates[ci * blk:(ci + 1) * blk, :]
        carry = carry + jnp.sum(mc, axis=0, keepdims=True)
    cnt_ref[...] = carry.astype(I32)
    pad_s[:, 2 * N_EXPERTS:] = jnp.zeros((s_tok, LANE - 2 * N_EXPERTS), F32)
    tr = pad_s[...].T
    rt_ref[...] = tr[0:N_EXPERTS, :].astype(I32)
    gt_ref[...] = tr[N_EXPERTS:2 * N_EXPERTS, :]


def _route(logits, s_tok):
    n = logits.shape[0]
    n_super = n // s_tok
    return pl.pallas_call(
        _route_kernel,
        grid=(n_super,),
        in_specs=[pl.BlockSpec((s_tok, N_EXPERTS), lambda b: (b, 0))],
        out_specs=[pl.BlockSpec((None, N_EXPERTS, s_tok), lambda b: (b, 0, 0)),
                   pl.BlockSpec((None, N_EXPERTS, s_tok), lambda b: (b, 0, 0)),
                   pl.BlockSpec((None, 1, N_EXPERTS), lambda b: (b, 0, 0))],
        out_shape=[jax.ShapeDtypeStruct((n_super, N_EXPERTS, s_tok), I32),
                   jax.ShapeDtypeStruct((n_super, N_EXPERTS, s_tok), F32),
                   jax.ShapeDtypeStruct((n_super, 1, N_EXPERTS), I32)],
        scratch_shapes=[pltpu.VMEM((s_tok, LANE), F32)],
        compiler_params=_params(("parallel",)),
        name="route",
    )(logits)


def _moe_kernel(cnt_ref, x_ref, rt_ref, gt_ref, wgu_ref, bgu_ref, wdn_ref, bdn_ref, o_ref):
    b = pl.program_id(0)
    e = pl.program_id(1)
    group, _, s_tok = rt_ref.shape
    m = MOE_ROWS

    @pl.when(e == 0)
    def _():
        o_ref[...] = jnp.zeros(o_ref.shape, F32)

    sub_iota = lax.broadcasted_iota(I32, (MXU_DIM, s_tok), 0)
    pad_rows = jnp.zeros((MXU_DIM - m, D_MODEL), BF16)

    for g in range(group):
        rows = slice(g * s_tok, (g + 1) * s_tok)
        cnt = cnt_ref[(b * group + g) * N_EXPERTS + e]
        rank_row = rt_ref[g, pl.ds(e, 1), :]
        gate_row = gt_ref[g, pl.ds(e, 1), :]

        def one_pass(p, carry, rows=rows, rank_row=rank_row, gate_row=gate_row):
            hit = (rank_row - p * m) == sub_iota
            pick = jnp.where(hit, 1.0, 0.0).astype(BF16)
            xs = _dot(pick[0:m, :], x_ref[rows, :]).astype(BF16)
            hgu = _dot(xs, wgu_ref[...]) + bgu_ref[...]
            gate = jnp.minimum(hgu[:, :D_FF], SWIGLU_LIMIT)
            up = jnp.clip(hgu[:, D_FF:], -SWIGLU_LIMIT, SWIGLU_LIMIT)
            hdn = (up + 1.0) * (gate * _sigmoid(SWIGLU_ALPHA * gate))
            y = _dot(hdn.astype(BF16), wdn_ref[...]) + bdn_ref[...]
            g_rows = jnp.sum(jnp.where(hit[0:m, :], gate_row, 0.0), axis=1, keepdims=True)
            ys = jnp.concatenate([(y * g_rows).astype(BF16), pad_rows], axis=0)
            o_ref[rows, :] += _dot_tn(pick, ys)
            return carry

        lax.fori_loop(0, (cnt + m - 1) // m, one_pass, 0)


def _moe(layer, counts, x1b, rt, gt, wgu, bgu, wdn, bdn):
    n = x1b.shape[0]
    n_super, _, s_tok = rt.shape
    group = MOE_GROUP if n_super % MOE_GROUP == 0 else 1
    return pl.pallas_call(
        _moe_kernel,
        grid_spec=pltpu.PrefetchScalarGridSpec(
            num_scalar_prefetch=1,
            grid=(n_super // group, N_EXPERTS),
            in_specs=[pl.BlockSpec((group * s_tok, D_MODEL), lambda b, e, c: (b, 0)),
                      pl.BlockSpec((group, N_EXPERTS, s_tok), lambda b, e, c: (b, 0, 0)),
                      pl.BlockSpec((group, N_EXPERTS, s_tok), lambda b, e, c: (b, 0, 0)),
                      pl.BlockSpec((None, None, D_MODEL, 2 * D_FF),
                                   lambda b, e, c: (layer, e, 0, 0)),
                      pl.BlockSpec((None, None, 1, 2 * D_FF), lambda b, e, c: (layer, e, 0, 0)),
                      pl.BlockSpec((None, None, D_FF, D_MODEL), lambda b, e, c: (layer, e, 0, 0)),
                      pl.BlockSpec((None, None, 1, D_MODEL), lambda b, e, c: (layer, e, 0, 0))],
            out_specs=pl.BlockSpec((group * s_tok, D_MODEL), lambda b, e, c: (b, 0))),
        out_shape=jax.ShapeDtypeStruct((n, D_MODEL), F32),
        compiler_params=_params(("parallel", "arbitrary")),
        name="moe",
    )(counts, x1b, rt, gt, wgu, bgu, wdn, bdn)


def _ln2_kernel(x_ref, f_ref, g_ref, b_ref, o_ref):
    o_ref[...] = _layernorm(DEEPNORM_ALPHA * x_ref[...] + f_ref[...], g_ref[...], b_ref[...])


def _ln2(x1, ffn, g, b):
    n = x1.shape[0]
    tm = ROW_TILE
    row = pl.BlockSpec((tm, D_MODEL), lambda i: (i, 0))
    const = pl.BlockSpec((1, D_MODEL), lambda i: (0, 0))
    return pl.pallas_call(
        _ln2_kernel,
        grid=(n // tm,),
        in_specs=[row, row, const, const],
        out_specs=row,
        out_shape=jax.ShapeDtypeStruct((n, D_MODEL), F32),
        compiler_params=_params(("parallel",)),
        name="ln2",
    )(x1, ffn, g, b)


def _permute_w_in(w_in):
    depth = w_in.shape[0]
    o_db = 4 * DN_QK
    o_da = o_db + DN_HEADS
    o_cq = o_da + DN_HEADS
    o_ckv = o_cq + SA_Q_RANK
    o_ik = o_ckv + SA_KV_RANK
    o_iw = o_ik + IDX_DIM
    o_end = o_iw + IDX_HEADS
    parts = [w_in[..., :o_db], w_in[..., o_cq:o_ckv], w_in[..., o_ckv:o_ik], w_in[..., o_ik:o_iw],
             w_in[..., o_iw:o_end], w_in[..., o_db:o_da], w_in[..., o_da:o_cq]]
    width = sum(p.shape[-1] for p in parts)
    parts.append(jnp.zeros((depth, D_MODEL, PROJ_WIDTH - width), w_in.dtype))
    return jnp.concatenate(parts, axis=-1).astype(BF16)


def kernel(x, w_in, dn_conv, dn_a_log, dn_dt_bias, dn_norm, sa_q_norm, sa_w_uq, sa_kv_norm, sa_w_uk, sa_w_uv, idx_w_q, idx_k_norm_g, idx_k_norm_b, w_o, ln1_g, ln1_b, router_w, router_b, w_gate_up, b_gate_up, w_down, b_down, ln2_g, ln2_b):
    batch, t, d = x.shape
    depth = w_in.shape[0]
    n = batch * t
    nc = t // DN_CHUNK
    s_moe = min(MOE_TOKENS, n)

    w_in_p = _permute_w_in(w_in)
    w_uq = sa_w_uq.astype(BF16)
    w_uk = sa_w_uk.astype(BF16)
    w_qidx = idx_w_q.astype(BF16)
    w_o_b = w_o.astype(BF16)
    eye_h = jnp.eye(SA_HEADS, dtype=F32)
    wuv_bd = jnp.einsum('lhrv,hg->lhrgv', sa_w_uv, eye_h).reshape(
        depth, SA_HEADS * SA_KV_RANK, SA_WIDTH).astype(BF16)
    wgu = w_gate_up.astype(BF16)
    wdn = w_down.astype(BF16)
    dn_sm = jnp.stack([dn_a_log, dn_dt_bias], axis=1)

    x2 = x.reshape(n, d)
    for l in range(depth):
        proj = _inproj(l, x2, w_in_p)
        gates = proj[:, COL_MISC + MISC_DB:COL_MISC + MISC_DA + DN_HEADS]
        gates = gates.reshape(batch, nc, DN_CHUNK, 2, DN_HEADS).transpose(0, 4, 3, 1, 2)
        ydn = _deltanet(proj, gates, dn_sm[l], dn_conv[l], dn_norm[l][None, :], batch, t)
        qlat, qidx, ckvn, kidx = _sa_prep(
            proj, sa_q_norm[l][None, :], w_uq[l], w_uk[l], w_qidx[l], sa_kv_norm[l][None, :],
            idx_k_norm_g[l][None, :], idx_k_norm_b[l][None, :])
        olat = _sparse_attention(qidx, qlat, proj, kidx, ckvn, batch, t)
        x1, x1b, logits = _outproj(
            ydn, olat, x2, w_o_b[l, :DN_WIDTH], w_o_b[l, DN_WIDTH:], wuv_bd[l],
            ln1_g[l][None, :], ln1_b[l][None, :], router_w[l], router_b[l][None, :])
        rt, gt, cnt = _route(logits, s_moe)
        ffn = _moe(l, cnt.reshape(-1), x1b, rt, gt, wgu, b_gate_up[:, :, None, :], wdn,
                   b_down[:, :, None, :])
        x2 = _ln2(x1, ffn, ln2_g[l][None, :], ln2_b[l][None, :])
    return x2.reshape(batch, t, d)
```

```python
import functools

import jax
import jax.numpy as jnp
from jax import lax
from jax.experimental import pallas as pl
from jax.experimental.pallas import tpu as pltpu

F32 = jnp.float32
BF16 = jnp.bfloat16
I32 = jnp.int32

D_MODEL = 1024
DN_HEADS = 4
DN_DK = 128
DN_CONV = 4
DN_QK = DN_HEADS * DN_DK
DN_WIDTH = DN_HEADS * DN_DK
SA_HEADS = 8
SA_DK = 64
SA_DV = 64
SA_Q_RANK = 256
SA_KV_RANK = 128
SA_WIDTH = SA_HEADS * SA_DV
IDX_HEADS = 8
IDX_DIM = 64
TOPK_MAX = 256
N_EXPERTS = 32
TOP_K = 4
D_FF = 1024
SWIGLU_LIMIT = 7.0
SWIGLU_ALPHA = 1.702
EPS = 1e-6
DEPTH = 4
DEEPNORM_ALPHA = (2 * DEPTH) ** 0.25

LANE = 128
MXU_DIM = 256
DN_CHUNK = 128
DN_PAIR = 2
PROJ_WIDTH = 2560
COL_Z = 3 * DN_QK
COL_CQ = COL_Z + DN_WIDTH
COL_CKV = COL_CQ + SA_Q_RANK
COL_MISC = COL_CKV + SA_KV_RANK
MISC_IW = IDX_DIM
MISC_DB = MISC_IW + IDX_HEADS
MISC_DA = MISC_DB + DN_HEADS
ROW_TILE = 512
Q_TILE = 256
SA_CLASSES = 8
MOE_TOKENS = 1024
MOE_ROWS = 160
MOE_GROUP = 2
INT_MIN = -(2 ** 31)
LOG2E = 1.4426950408889634
VMEM_LIMIT = 56 * 1024 * 1024

_HI = lax.Precision.HIGHEST


def _dot(a, b, precision=None):
    return jnp.dot(a, b, preferred_element_type=F32, precision=precision)


def _dot_nt(a, b, precision=None):
    return lax.dot_general(a, b, (((1,), (1,)), ((), ())), preferred_element_type=F32,
                           precision=precision)


def _dot_tn(a, b):
    return lax.dot_general(a, b, (((0,), (0,)), ((), ())), preferred_element_type=F32)


def _split(x):
    hi = x.astype(BF16)
    return hi, (x - hi.astype(F32)).astype(BF16)


def _sigmoid(x):
    return 1.0 / (1.0 + jnp.exp(-x))


def _silu(x):
    return x * _sigmoid(x)


def _softplus(x):
    return jnp.maximum(x, 0.0) + jnp.log(1.0 + jnp.exp(-jnp.abs(x)))


def _layernorm(x, g, b):
    mu = jnp.mean(x, axis=-1, keepdims=True)
    xc = x - mu
    var = jnp.mean(xc * xc, axis=-1, keepdims=True)
    return xc * lax.rsqrt(var + EPS) * g + b


def _rmsnorm(x, g):
    return x * lax.rsqrt(jnp.mean(x * x, axis=-1, keepdims=True) + EPS) * g


def _params(sem):
    return pltpu.CompilerParams(dimension_semantics=sem, vmem_limit_bytes=VMEM_LIMIT)


def _inproj_kernel(x_ref, w_ref, o_ref):
    o_ref[...] = _dot(x_ref[...].astype(BF16), w_ref[...])


def _inproj(layer, x2, w):
    n = x2.shape[0]
    return pl.pallas_call(
        _inproj_kernel,
        grid=(n // ROW_TILE,),
        in_specs=[pl.BlockSpec((ROW_TILE, D_MODEL), lambda i: (i, 0)),
                  pl.BlockSpec((None, D_MODEL, PROJ_WIDTH), lambda i: (layer, 0, 0))],
        out_specs=pl.BlockSpec((ROW_TILE, PROJ_WIDTH), lambda i: (i, 0)),
        out_shape=jax.ShapeDtypeStruct((n, PROJ_WIDTH), F32),
        compiler_params=_params(("parallel",)),
        name="inproj",
    )(x2, w)


def _ln2_inproj_kernel(x_ref, f_ref, g_ref, b_ref, w_ref, x2_ref, o_ref):
    x2 = _layernorm(DEEPNORM_ALPHA * x_ref[...] + f_ref[...], g_ref[...], b_ref[...])
    x2_ref[...] = x2
    o_ref[...] = _dot(x2.astype(BF16), w_ref[...])


def _ln2_inproj(layer, x1, ffn, g, b, w):
    n = x1.shape[0]
    row = pl.BlockSpec((ROW_TILE, D_MODEL), lambda i: (i, 0))
    const = pl.BlockSpec((1, D_MODEL), lambda i: (0, 0))
    return pl.pallas_call(
        _ln2_inproj_kernel,
        grid=(n // ROW_TILE,),
        in_specs=[row, row, const, const,
                  pl.BlockSpec((None, D_MODEL, PROJ_WIDTH), lambda i: (layer, 0, 0))],
        out_specs=[row, pl.BlockSpec((ROW_TILE, PROJ_WIDTH), lambda i: (i, 0))],
        out_shape=[jax.ShapeDtypeStruct((n, D_MODEL), F32),
                   jax.ShapeDtypeStruct((n, PROJ_WIDTH), F32)],
        compiler_params=_params(("parallel",)),
        name="ln2_inproj",
    )(x1, ffn, g, b, w)


def _dn_kernel(sm_ref, q_ref, k_ref, v_ref, z_ref, g_ref, wq_ref, wk_ref, wv_ref, nw_ref,
               o_ref, beta_s, gcum_s):
    hp = pl.program_id(1)
    t = q_ref.shape[0]
    c = DN_CHUNK
    n_chunks = t // c
    ii = lax.broadcasted_iota(I32, (c, c), 0)
    jj = lax.broadcasted_iota(I32, (c, c), 1)
    lower_incl = ii >= jj
    cum_mat = jnp.where(ii <= jj, 1.0, 0.0).astype(F32)
    nw = nw_ref[...]

    def conv_silu(x_ref, w_ref, ci, r0, sl):
        w = w_ref[:, sl]
        cur = x_ref[pl.ds(r0, c), sl]
        before = x_ref[pl.ds(pl.multiple_of(jnp.maximum(r0 - 8, 0), 8), 8), sl]
        both = jnp.concatenate([jnp.where(ci > 0, before, 0.0), cur], axis=0)
        y = cur * w[DN_CONV - 1:DN_CONV, :]
        for s in range(1, DN_CONV):
            y = y + both[8 - s:8 - s + c, :] * w[DN_CONV - 1 - s:DN_CONV - s, :]
        return _silu(y)

    def l2norm(x):
        return x * lax.rsqrt(jnp.sum(x * x, axis=-1, keepdims=True) + EPS)

    for hh in range(DN_PAIR):
        sl = slice(hh * LANE, (hh + 1) * LANE)
        head = hp * DN_PAIR + hh
        a_coef = jnp.exp(jnp.full((1, LANE), sm_ref[0, head], F32))
        beta_s[hh] = _sigmoid(g_ref[hh, 0])
        g_log = -a_coef * _softplus(g_ref[hh, 1] + sm_ref[1, head])
        gcum_s[hh] = _dot(g_log, cum_mat, precision=_HI)

    def local(ci):
        r0 = pl.multiple_of(ci * c, c)
        a_pows, sols, outs = [], [], []
        for hh in range(DN_PAIR):
            g_row = jnp.broadcast_to(gcum_s[hh, pl.ds(ci, 1), :], (c, c))
            g_col = g_row.T
            b_col = jnp.broadcast_to(beta_s[hh, pl.ds(ci, 1), :], (c, c)).T
            g_last = jnp.broadcast_to(g_row[:, c - 1:c], (c, c))
            decay = jnp.exp(jnp.where(lower_incl, g_col - g_row, -jnp.inf))
            sl = slice(hh * LANE, (hh + 1) * LANE)
            qc = l2norm(conv_silu(q_ref, wq_ref, ci, r0, sl)) * (DN_DK ** -0.5)
            kc = l2norm(conv_silu(k_ref, wk_ref, ci, r0, sl))
            vc = conv_silu(v_ref, wv_ref, ci, r0, sl)
            kb = kc * b_col
            kcb = kc.astype(BF16)
            a_pows.append(jnp.where(ii > jj, -(_dot_nt(kb.astype(BF16), kcb) * decay), 0.0))
            sols.append(jnp.concatenate([vc * b_col, kb * jnp.exp(g_col)], axis=1))
            outs.append(dict(
                attn=(_dot_nt(qc.astype(BF16), kcb) * decay).astype(BF16),
                k_dec_t=(kc * jnp.exp(g_last - g_col)).T.astype(BF16),
                q_dec=(qc * jnp.exp(g_col)).astype(BF16),
                keep=jnp.exp(g_last)))

        n_steps = (c - 1).bit_length()
        for step in range(n_steps):
            last = step + 1 == n_steps
            rhs = jnp.concatenate(
                [sols[hh] if last else jnp.concatenate([sols[hh], a_pows[hh]], axis=1)
                 for hh in range(DN_PAIR)], axis=0)
            a_bd = jnp.concatenate(
                [jnp.concatenate([a_pows[hh] if g == hh else jnp.zeros((c, c), F32)
                                  for g in range(DN_PAIR)], axis=1)
                 for hh in range(DN_PAIR)], axis=0)
            a_hi, a_lo = _split(a_bd)
            r_hi, r_lo = _split(rhs)
            both = _dot(a_hi, jnp.concatenate([r_hi, r_lo], axis=1))
            prod = both[:, 0:rhs.shape[1]] + (both[:, rhs.shape[1]:] + _dot(a_lo, r_hi))
            for hh in range(DN_PAIR):
                mine = prod[hh * c:(hh + 1) * c, :]
                sols[hh] = sols[hh] + mine[:, 0:2 * c]
                if not last:
                    a_pows[hh] = mine[:, 2 * c:]
        for hh in range(DN_PAIR):
            outs[hh]["u"] = sols[hh][:, :c]
            outs[hh]["w"] = sols[hh][:, c:].astype(BF16)
        return outs

    def recur(ci, states, loc):
        r0 = pl.multiple_of(ci * c, c)
        new_states = []
        for hh in range(DN_PAIR):
            sl = slice(hh * LANE, (hh + 1) * LANE)
            state = states[hh]
            sb = state.astype(BF16)
            v_new = loc[hh]["u"] - _dot(loc[hh]["w"], sb)
            vb = v_new.astype(BF16)
            o = _dot(loc[hh]["q_dec"], sb) + _dot(loc[hh]["attn"], vb)
            new_states.append(state * loc[hh]["keep"] + _dot(loc[hh]["k_dec_t"], vb))
            o_ref[pl.ds(r0, c), sl] = _rmsnorm(o, nw) * _silu(z_ref[pl.ds(r0, c), sl])
        return tuple(new_states)

    def step(ci, carry):
        states, loc = carry
        return recur(ci - 1, states, loc), local(ci)

    init = tuple(jnp.zeros((c, c), F32) for _ in range(DN_PAIR))
    states, loc = lax.fori_loop(1, n_chunks, step, (init, local(0)), unroll=2)
    recur(n_chunks - 1, states, loc)


def _deltanet(proj, gates, dn_sm, conv_w, norm_w, batch, t):
    n = proj.shape[0]
    nc = t // DN_CHUNK
    width = DN_PAIR * LANE
    n_pairs = DN_HEADS // DN_PAIR
    blk = lambda off: pl.BlockSpec((t, width), lambda b, h: (b, off + h))
    wblk = lambda off: pl.BlockSpec((DN_CONV, width), lambda b, h: (0, off + h))
    per_chunk = pltpu.VMEM((DN_PAIR, nc, LANE), F32)
    return pl.pallas_call(
        _dn_kernel,
        grid=(batch, n_pairs),
        in_specs=[pl.BlockSpec(memory_space=pltpu.SMEM),
                  blk(0), blk(n_pairs), blk(2 * n_pairs), blk(3 * n_pairs),
                  pl.BlockSpec((None, DN_PAIR, 2, nc, LANE), lambda b, h: (b, h, 0, 0, 0)),
                  wblk(0), wblk(n_pairs), wblk(2 * n_pairs),
                  pl.BlockSpec((1, LANE), lambda b, h: (0, 0))],
        out_specs=pl.BlockSpec((t, width), lambda b, h: (b, h)),
        out_shape=jax.ShapeDtypeStruct((n, DN_WIDTH), F32),
        scratch_shapes=[per_chunk, per_chunk],
        compiler_params=_params(("parallel", "parallel")),
        name="deltanet",
    )(dn_sm, proj, proj, proj, proj, gates, conv_w, conv_w, conv_w, norm_w)


def _sa_prep_kernel(cq_ref, ckv_ref, misc_ref, qn_ref, wuq_ref, wuk_ref, wqi_ref, kvn_ref,
                    ig_ref, ib_ref, qlat_ref, qidx_ref, ckvn_ref, kidx_ref):
    cq = _rmsnorm(cq_ref[...], qn_ref[...]).astype(BF16)
    q = _dot(cq, wuq_ref[...])
    qi = _dot(cq, wqi_ref[...])
    for h in range(SA_HEADS):
        qh = q[:, h * SA_DK:(h + 1) * SA_DK].astype(BF16)
        ql = _dot(qh, wuk_ref[h]) * ((SA_DK ** -0.5) * LOG2E)
        qlat_ref[:, h * SA_KV_RANK:(h + 1) * SA_KV_RANK] = ql.astype(BF16)
    for h in range(IDX_HEADS):
        qidx_ref[h] = qi[:, h * IDX_DIM:(h + 1) * IDX_DIM].astype(BF16)
    ckvn_ref[...] = _rmsnorm(ckv_ref[...], kvn_ref[...]).astype(BF16)
    ik = misc_ref[...][:, :IDX_DIM]
    kidx_ref[...] = _layernorm(ik, ig_ref[...], ib_ref[...]).astype(BF16)


def _sa_prep(proj, q_norm, w_uq, w_uk, w_qidx, kv_norm, ig, ib):
    n = proj.shape[0]
    tm = ROW_TILE
    const2 = lambda shp: pl.BlockSpec(shp, lambda i: (0, 0))
    return pl.pallas_call(
        _sa_prep_kernel,
        grid=(n // tm,),
        in_specs=[pl.BlockSpec((tm, SA_Q_RANK), lambda i: (i, COL_CQ // SA_Q_RANK)),
                  pl.BlockSpec((tm, SA_KV_RANK), lambda i: (i, COL_CKV // SA_KV_RANK)),
                  pl.BlockSpec((tm, LANE), lambda i: (i, COL_MISC // LANE)),
                  const2((1, SA_Q_RANK)),
                  const2((SA_Q_RANK, SA_HEADS * SA_DK)),
                  pl.BlockSpec((SA_HEADS, SA_DK, SA_KV_RANK), lambda i: (0, 0, 0)),
                  const2((SA_Q_RANK, IDX_HEADS * IDX_DIM)),
                  const2((1, SA_KV_RANK)),
                  const2((1, IDX_DIM)), const2((1, IDX_DIM))],
        out_specs=[pl.BlockSpec((tm, SA_HEADS * SA_KV_RANK), lambda i: (i, 0)),
                   pl.BlockSpec((IDX_HEADS, tm, IDX_DIM), lambda i: (0, i, 0)),
                   pl.BlockSpec((tm, SA_KV_RANK), lambda i: (i, 0)),
                   pl.BlockSpec((tm, IDX_DIM), lambda i: (i, 0))],
        out_shape=[jax.ShapeDtypeStruct((n, SA_HEADS * SA_KV_RANK), BF16),
                   jax.ShapeDtypeStruct((IDX_HEADS, n, IDX_DIM), BF16),
                   jax.ShapeDtypeStruct((n, SA_KV_RANK), BF16),
                   jax.ShapeDtypeStruct((n, IDX_DIM), BF16)],
        compiler_params=_params(("parallel",)),
        name="sa_prep",
    )(proj, proj, proj, q_norm, w_uq, w_uk, w_qidx, kv_norm, ig, ib)


def _sa_body(j, qidx_ref, qlat_ref, misc_ref, kidx_ref, ckv_ref, o_ref, key_s, tie_s, bound_s,
             *, k_top, t_eff):
    qb = qlat_ref.shape[0]
    kidx = kidx_ref[0:t_eff, :]
    w_rows = misc_ref[...].T * ((IDX_HEADS ** -0.5) * (IDX_DIM ** -0.5))
    logits = jnp.maximum(_dot_nt(kidx, qidx_ref[...].reshape(IDX_HEADS * qb, IDX_DIM)), 0.0)
    score = jnp.zeros((t_eff, qb), F32)
    for h in range(IDX_HEADS):
        score = score + w_rows[MISC_IW + h:MISC_IW + h + 1, :] * logits[:, h * qb:(h + 1) * qb]

    key_pos = lax.broadcasted_iota(I32, (t_eff, qb), 0)
    t_pos = j * qb + lax.broadcasted_iota(I32, (t_eff, qb), 1)
    causal = key_pos <= t_pos
    bits = pltpu.bitcast(score, I32)
    key = bits ^ ((bits >> 31) & 0x7FFFFFFF)
    key_s[0:t_eff, :] = jnp.where(causal, key, INT_MIN)

    def count(flags):
        ways = 8
        accs = [flags[g * 8:(g + 1) * 8, :] for g in range(ways)]
        for r in range(ways, t_eff // 8):
            accs[r % ways] = accs[r % ways] + flags[r * 8:(r + 1) * 8, :]
        while len(accs) > 1:
            accs = [a + b for a, b in zip(accs[0::2], accs[1::2])]
        return jnp.sum(accs[0], axis=0, keepdims=True)

    def thr_step(i, thr_u):
        cand_u = thr_u | lax.shift_left(jnp.int32(1), 31 - i)
        cnt = count(jnp.where(key_s[0:t_eff, :] >= (cand_u ^ INT_MIN), 1.0, 0.0))
        return jnp.where(cnt >= k_top, cand_u, thr_u)

    thr = lax.fori_loop(0, 32, thr_step, jnp.zeros((1, qb), I32), unroll=2) ^ INT_MIN
    key = key_s[0:t_eff, :]
    above = key > thr
    tie = jnp.where(causal, jnp.where(key == thr, 1.0, 0.0), 0.0).astype(F32)
    tie_s[0:t_eff, :] = tie
    need = k_top - count(jnp.where(above, 1.0, 0.0))
    excess = count(tie) - need

    n_bits = t_eff.bit_length()
    bound_s[...] = jnp.full((1, qb), 2 ** n_bits - 1, I32)

    @pl.when(jnp.max(excess) > 0.0)
    def _():
        def tie_step(i, bound):
            cand = bound | lax.shift_left(jnp.int32(1), (n_bits - 1) - i)
            cnt = count(jnp.where(key_pos < cand, tie_s[0:t_eff, :], 0.0))
            return jnp.where(cnt <= need, cand, bound)

        bound_s[...] = lax.fori_loop(0, n_bits, tie_step, jnp.zeros((1, qb), I32))

    chosen = jnp.where(above, 1.0, jnp.where(key_pos < bound_s[...], tie_s[0:t_eff, :], 0.0))
    bias = jnp.where(chosen > 0.0, 0.0, -jnp.inf).astype(F32).T

    ckv = ckv_ref[0:t_eff, :]
    for h in range(SA_HEADS):
        s = _dot_nt(qlat_ref[:, h * SA_KV_RANK:(h + 1) * SA_KV_RANK], ckv) + bias
        m = jnp.max(s, axis=1, keepdims=True)
        p = jnp.exp2(s - m)
        l = jnp.sum(p, axis=1, keepdims=True)
        o = _dot(p.astype(BF16), ckv) / l
        o_ref[:, h * SA_KV_RANK:(h + 1) * SA_KV_RANK] = o.astype(BF16)


def _sa_kernel(qidx_ref, qlat_ref, misc_ref, kidx_ref, ckv_ref, o_ref, key_s, tie_s, bound_s,
               *, k_top, n_cls):
    qb = qlat_ref.shape[0]
    t = kidx_ref.shape[0]
    j = pl.program_id(1)
    per = (t // qb) // n_cls
    for cls in range(n_cls):
        body = functools.partial(_sa_body, j, qidx_ref, qlat_ref, misc_ref, kidx_ref, ckv_ref,
                                 o_ref, key_s, tie_s, bound_s, k_top=k_top,
                                 t_eff=(cls + 1) * per * qb)
        pl.when(j // per == cls)(body)


def _sparse_attention(qidx, qlat, proj, kidx, ckvn, batch, t):
    n = proj.shape[0]
    nq = t // Q_TILE
    k_top = min(TOPK_MAX, t // 4)
    n_cls = SA_CLASSES if nq % SA_CLASSES == 0 else 1
    return pl.pallas_call(
        functools.partial(_sa_kernel, k_top=k_top, n_cls=n_cls),
        grid=(batch, nq),
        in_specs=[pl.BlockSpec((IDX_HEADS, Q_TILE, IDX_DIM), lambda b, j: (0, b * nq + j, 0)),
                  pl.BlockSpec((Q_TILE, SA_HEADS * SA_KV_RANK), lambda b, j: (b * nq + j, 0)),
                  pl.BlockSpec((Q_TILE, LANE), lambda b, j: (b * nq + j, COL_MISC // LANE)),
                  pl.BlockSpec((t, IDX_DIM), lambda b, j: (b, 0)),
                  pl.BlockSpec((t, SA_KV_RANK), lambda b, j: (b, 0))],
        out_specs=pl.BlockSpec((Q_TILE, SA_HEADS * SA_KV_RANK), lambda b, j: (b * nq + j, 0)),
        out_shape=jax.ShapeDtypeStruct((n, SA_HEADS * SA_KV_RANK), BF16),
        scratch_shapes=[pltpu.VMEM((t, Q_TILE), I32), pltpu.VMEM((t, Q_TILE), F32),
                        pltpu.VMEM((1, Q_TILE), I32)],
        compiler_params=_params(("parallel", "parallel")),
        name="sparse_attention",
    )(qidx, qlat, proj, kidx, ckvn)


def _outproj_kernel(ydn_ref, olat_ref, x_ref, wdn_ref, wsa_ref, wuv_ref, g_ref, b_ref, rw_ref,
                    rb_ref, x1_ref, x1b_ref, lg_ref):
    ysa = _dot(olat_ref[...], wuv_ref[...])
    mix = _dot(ydn_ref[...].astype(BF16), wdn_ref[...]) + _dot(ysa.astype(BF16), wsa_ref[...])
    x1 = _layernorm(DEEPNORM_ALPHA * x_ref[...] + mix, g_ref[...], b_ref[...])
    x1_ref[...] = x1
    x_hi, x_lo = _split(x1)
    w_hi, w_lo = _split(rw_ref[...])
    x1b_ref[...] = x_hi
    lg_ref[...] = _dot(x_hi, w_hi) + (_dot(x_hi, w_lo) + _dot(x_lo, w_hi)) + rb_ref[...]


def _outproj(ydn, olat, x2, w_o_dn, w_o_sa, wuv_bd, g, b, rw, rb):
    n = x2.shape[0]
    tm = ROW_TILE
    row = lambda w: pl.BlockSpec((tm, w), lambda i: (i, 0))
    const2 = lambda shp: pl.BlockSpec(shp, lambda i: (0, 0))
    return pl.pallas_call(
        _outproj_kernel,
        grid=(n // tm,),
        in_specs=[row(DN_WIDTH), row(SA_HEADS * SA_KV_RANK), row(D_MODEL),
                  const2((DN_WIDTH, D_MODEL)), const2((SA_WIDTH, D_MODEL)),
                  const2((SA_HEADS * SA_KV_RANK, SA_WIDTH)),
                  const2((1, D_MODEL)), const2((1, D_MODEL)),
                  const2((D_MODEL, N_EXPERTS)), const2((1, N_EXPERTS))],
        out_specs=[row(D_MODEL), row(D_MODEL), row(N_EXPERTS)],
        out_shape=[jax.ShapeDtypeStruct((n, D_MODEL), F32),
                   jax.ShapeDtypeStruct((n, D_MODEL), BF16),
                   jax.ShapeDtypeStruct((n, N_EXPERTS), F32)],
        compiler_params=_params(("parallel",)),
        name="outproj",
    )(ydn, olat, x2, w_o_dn, w_o_sa, wuv_bd, g, b, rw, rb)


def _route_kernel(lg_ref, rt_ref, gt_ref, cnt_ref, pad_s):
    s_tok = lg_ref.shape[0]
    l = lg_ref[...]
    lane = lax.broadcasted_iota(I32, l.shape, 1)
    vals, hots = [], []
    for _ in range(TOP_K):
        m = jnp.max(l, axis=1, keepdims=True)
        idx = jnp.min(jnp.where(l == m, lane, N_EXPERTS), axis=1, keepdims=True)
        hot = lane == idx
        vals.append(m)
        hots.append(hot)
        l = jnp.where(hot, -jnp.inf, l)
    exps = [jnp.exp(v - vals[0]) for v in vals]
    den = exps[0]
    for e in exps[1:]:
        den = den + e
    gates = jnp.zeros(l.shape, F32)
    chosen = jnp.zeros(l.shape, F32)
    for hot, e in zip(hots, exps):
        gates = gates + jnp.where(hot, e / den, 0.0)
        chosen = chosen + jnp.where(hot, 1.0, 0.0)

    blk = 256
    ii = lax.broadcasted_iota(I32, (blk, blk), 0)
    jj = lax.broadcasted_iota(I32, (blk, blk), 1)
    tri = jnp.where(ii > jj, 1.0, 0.0).astype(BF16)
    carry = jnp.zeros((1, N_EXPERTS), F32)
    for ci in range(s_tok // blk):
        mc = chosen[ci * blk:(ci + 1) * blk, :]
        rk = _dot(tri, mc.astype(BF16)) + carry
        rank = jnp.where(mc > 0.0, rk, -1.0)
        pad_s[ci * blk:(ci + 1) * blk, 0:N_EXPERTS] = rank
        pad_s[ci * blk:(ci + 1) * blk, N_EXPERTS:2 * N_EXPERTS] = gates[ci * blk:(ci + 1) * blk, :]
        carry = carry + jnp.sum(mc, axis=0, keepdims=True)
    cnt_ref[...] = carry.astype(I32)
    pad_s[:, 2 * N_EXPERTS:] = jnp.zeros((s_tok, LANE - 2 * N_EXPERTS), F32)
    tr = pad_s[...].T
    rt_ref[...] = tr[0:N_EXPERTS, :].astype(I32)
    gt_ref[...] = tr[N_EXPERTS:2 * N_EXPERTS, :]


def _route(logits, s_tok):
    n = logits.shape[0]
    n_super = n // s_tok
    return pl.pallas_call(
        _route_kernel,
        grid=(n_super,),
        in_specs=[pl.BlockSpec((s_tok, N_EXPERTS), lambda b: (b, 0))],
        out_specs=[pl.BlockSpec((None, N_EXPERTS, s_tok), lambda b: (b, 0, 0)),
                   pl.BlockSpec((None, N_EXPERTS, s_tok), lambda b: (b, 0, 0)),
                   pl.BlockSpec((None, 1, N_EXPERTS), lambda b: (b, 0, 0))],
        out_shape=[jax.ShapeDtypeStruct((n_super, N_EXPERTS, s_tok), I32),
                   jax.ShapeDtypeStruct((n_super, N_EXPERTS, s_tok), F32),
                   jax.ShapeDtypeStruct((n_super, 1, N_EXPERTS), I32)],
        scratch_shapes=[pltpu.VMEM((s_tok, LANE), F32)],
        compiler_params=_params(("parallel",)),
        name="route",
    )(logits)


def _moe_kernel(cnt_ref, x_ref, rt_ref, gt_ref, wgu_ref, bgu_ref, wdn_ref, bdn_ref, o_ref):
    b = pl.program_id(0)
    e = pl.program_id(1)
    group, _, s_tok = rt_ref.shape
    m = MOE_ROWS

    @pl.when(e == 0)
    def _():
        o_ref[...] = jnp.zeros(o_ref.shape, F32)

    sub_iota = lax.broadcasted_iota(I32, (MXU_DIM, s_tok), 0)
    pad_rows = jnp.zeros((MXU_DIM - m, D_MODEL), BF16)

    for g in range(group):
        rows = slice(g * s_tok, (g + 1) * s_tok)
        cnt = cnt_ref[(b * group + g) * N_EXPERTS + e]
        rank_row = rt_ref[g, pl.ds(e, 1), :]
        gate_row = gt_ref[g, pl.ds(e, 1), :]

        def one_pass(p, carry, rows=rows, rank_row=rank_row, gate_row=gate_row):
            hit = (rank_row - p * m) == sub_iota
            pick = jnp.where(hit, 1.0, 0.0).astype(BF16)
            xs = _dot(pick[0:m, :], x_ref[rows, :]).astype(BF16)
            hgu = _dot(xs, wgu_ref[...]) + bgu_ref[...]
            gate = jnp.minimum(hgu[:, :D_FF], SWIGLU_LIMIT)
            up = jnp.clip(hgu[:, D_FF:], -SWIGLU_LIMIT, SWIGLU_LIMIT)
            hdn = (up + 1.0) * (gate * _sigmoid(SWIGLU_ALPHA * gate))
            y = _dot(hdn.astype(BF16), wdn_ref[...]) + bdn_ref[...]
            g_rows = jnp.sum(jnp.where(hit[0:m, :], gate_row, 0.0), axis=1, keepdims=True)
            ys = jnp.concatenate([(y * g_rows).astype(BF16), pad_rows], axis=0)
            o_ref[rows, :] += _dot_tn(pick, ys)
            return carry

        lax.fori_loop(0, (cnt + m - 1) // m, one_pass, 0)


def _moe(layer, counts, x1b, rt, gt, wgu, bgu, wdn, bdn):
    n = x1b.shape[0]
    n_super, _, s_tok = rt.shape
    group = MOE_GROUP if n_super % MOE_GROUP == 0 else 1
    return pl.pallas_call(
        _moe_kernel,
        grid_spec=pltpu.PrefetchScalarGridSpec(
            num_scalar_prefetch=1,
            grid=(n_super // group, N_EXPERTS),
            in_specs=[pl.BlockSpec((group * s_tok, D_MODEL), lambda b, e, c: (b, 0)),
                      pl.BlockSpec((group, N_EXPERTS, s_tok), lambda b, e, c: (b, 0, 0)),
                      pl.BlockSpec((group, N_EXPERTS, s_tok), lambda b, e, c: (b, 0, 0)),
                      pl.BlockSpec((None, None, D_MODEL, 2 * D_FF),
                                   lambda b, e, c: (layer, e, 0, 0)),
                      pl.BlockSpec((None, None, 1, 2 * D_FF), lambda b, e, c: (layer, e, 0, 0)),
                      pl.BlockSpec((None, None, D_FF, D_MODEL), lambda b, e, c: (layer, e, 0, 0)),
                      pl.BlockSpec((None, None, 1, D_MODEL), lambda b, e, c: (layer, e, 0, 0))],
            out_specs=pl.BlockSpec((group * s_tok, D_MODEL), lambda b, e, c: (b, 0))),
        out_shape=jax.ShapeDtypeStruct((n, D_MODEL), F32),
        compiler_params=_params(("parallel", "arbitrary")),
        name="moe",
    )(counts, x1b, rt, gt, wgu, bgu, wdn, bdn)


def _ln2_kernel(x_ref, f_ref, g_ref, b_ref, o_ref):
    o_ref[...] = _layernorm(DEEPNORM_ALPHA * x_ref[...] + f_ref[...], g_ref[...], b_ref[...])


def _ln2(x1, ffn, g, b):
    n = x1.shape[0]
    tm = ROW_TILE
    row = pl.BlockSpec((tm, D_MODEL), lambda i: (i, 0))
    const = pl.BlockSpec((1, D_MODEL), lambda i: (0, 0))
    return pl.pallas_call(
        _ln2_kernel,
        grid=(n // tm,),
        in_specs=[row, row, const, const],
        out_specs=row,
        out_shape=jax.ShapeDtypeStruct((n, D_MODEL), F32),
        compiler_params=_params(("parallel",)),
        name="ln2",
    )(x1, ffn, g, b)


def _permute_w_in(w_in):
    depth = w_in.shape[0]
    o_db = 4 * DN_QK
    o_da = o_db + DN_HEADS
    o_cq = o_da + DN_HEADS
    o_ckv = o_cq + SA_Q_RANK
    o_ik = o_ckv + SA_KV_RANK
    o_iw = o_ik + IDX_DIM
    o_end = o_iw + IDX_HEADS
    parts = [w_in[..., :o_db], w_in[..., o_cq:o_ckv], w_in[..., o_ckv:o_ik], w_in[..., o_ik:o_iw],
             w_in[..., o_iw:o_end], w_in[..., o_db:o_da], w_in[..., o_da:o_cq]]
    width = sum(p.shape[-1] for p in parts)
    parts.append(jnp.zeros((depth, D_MODEL, PROJ_WIDTH - width), w_in.dtype))
    return jnp.concatenate(parts, axis=-1).astype(BF16)


def kernel(x, w_in, dn_conv, dn_a_log, dn_dt_bias, dn_norm, sa_q_norm, sa_w_uq, sa_kv_norm, sa_w_uk, sa_w_uv, idx_w_q, idx_k_norm_g, idx_k_norm_b, w_o, ln1_g, ln1_b, router_w, router_b, w_gate_up, b_gate_up, w_down, b_down, ln2_g, ln2_b):
    batch, t, d = x.shape
    depth = w_in.shape[0]
    n = batch * t
    nc = t // DN_CHUNK
    s_moe = min(MOE_TOKENS, n)

    w_in_p = _permute_w_in(w_in)
    w_uq = sa_w_uq.astype(BF16)
    w_uk = sa_w_uk.astype(BF16)
    w_qidx = idx_w_q.astype(BF16)
    w_o_b = w_o.astype(BF16)
    eye_h = jnp.eye(SA_HEADS, dtype=F32)
    wuv_bd = jnp.einsum('lhrv,hg->lhrgv', sa_w_uv, eye_h).reshape(
        depth, SA_HEADS * SA_KV_RANK, SA_WIDTH).astype(BF16)
    wgu = w_gate_up.astype(BF16)
    wdn = w_down.astype(BF16)
    dn_sm = jnp.stack([dn_a_log, dn_dt_bias], axis=1)

    x2 = x.reshape(n, d)
    x1 = ffn = None
    for l in range(depth):
        if l == 0:
            proj = _inproj(l, x2, w_in_p)
        else:
            x2, proj = _ln2_inproj(l, x1, ffn, ln2_g[l - 1][None, :], ln2_b[l - 1][None, :], w_in_p)
        gates = proj[:, COL_MISC + MISC_DB:COL_MISC + MISC_DA + DN_HEADS]
        gates = gates.reshape(batch, nc, DN_CHUNK, 2, DN_HEADS).transpose(0, 4, 3, 1, 2)
        ydn = _deltanet(proj, gates, dn_sm[l], dn_conv[l], dn_norm[l][None, :], batch, t)
        qlat, qidx, ckvn, kidx = _sa_prep(
            proj, sa_q_norm[l][None, :], w_uq[l], w_uk[l], w_qidx[l], sa_kv_norm[l][None, :],
            idx_k_norm_g[l][None, :], idx_k_norm_b[l][None, :])
        olat = _sparse_attention(qidx, qlat, proj, kidx, ckvn, batch, t)
        x1, x1b, logits = _outproj(
            ydn, olat, x2, w_o_b[l, :DN_WIDTH], w_o_b[l, DN_WIDTH:], wuv_bd[l],
            ln1_g[l][None, :], ln1_b[l][None, :], router_w[l], router_b[l][None, :])
        rt, gt, cnt = _route(logits, s_moe)
        ffn = _moe(l, cnt.reshape(-1), x1b, rt, gt, wgu, b_gate_up[:, :, None, :], wdn,
                   b_down[:, :, None, :])
    x2 = _ln2(x1, ffn, ln2_g[depth - 1][None, :], ln2_b[depth - 1][None, :])
    return x2.reshape(batch, t, d)
```

```python
import functools

import jax
import jax.numpy as jnp
from jax import lax
from jax.experimental import pallas as pl
from jax.experimental.pallas import tpu as pltpu

F32 = jnp.float32
BF16 = jnp.bfloat16
I32 = jnp.int32

D_MODEL = 1024
DN_HEADS = 4
DN_DK = 128
DN_CONV = 4
DN_QK = DN_HEADS * DN_DK
DN_WIDTH = DN_HEADS * DN_DK
SA_HEADS = 8
SA_DK = 64
SA_DV = 64
SA_Q_RANK = 256
SA_KV_RANK = 128
SA_WIDTH = SA_HEADS * SA_DV
IDX_HEADS = 8
IDX_DIM = 64
TOPK_MAX = 256
N_EXPERTS = 32
TOP_K = 4
D_FF = 1024
SWIGLU_LIMIT = 7.0
SWIGLU_ALPHA = 1.702
EPS = 1e-6
DEPTH = 4
DEEPNORM_ALPHA = (2 * DEPTH) ** 0.25

LANE = 128
MXU_DIM = 256
DN_CHUNK = 128
DN_PAIR = 2
PROJ_WIDTH = 2560
COL_Z = 3 * DN_QK
COL_CQ = COL_Z + DN_WIDTH
COL_CKV = COL_CQ + SA_Q_RANK
COL_MISC = COL_CKV + SA_KV_RANK
MISC_IW = IDX_DIM
MISC_DB = MISC_IW + IDX_HEADS
MISC_DA = MISC_DB + DN_HEADS
ROW_TILE = 512
Q_TILE = 128
SA_CLASSES = 8
MOE_TOKENS = 1024
MOE_ROWS = 160
MOE_GROUP = 2
INT_MIN = -(2 ** 31)
LOG2E = 1.4426950408889634
VMEM_LIMIT = 56 * 1024 * 1024

_HI = lax.Precision.HIGHEST


def _dot(a, b, precision=None):
    return jnp.dot(a, b, preferred_element_type=F32, precision=precision)


def _dot_nt(a, b, precision=None):
    return lax.dot_general(a, b, (((1,), (1,)), ((), ())), preferred_element_type=F32,
                           precision=precision)


def _dot_tn(a, b):
    return lax.dot_general(a, b, (((0,), (0,)), ((), ())), preferred_element_type=F32)


def _split(x):
    hi = x.astype(BF16)
    return hi, (x - hi.astype(F32)).astype(BF16)


def _sigmoid(x):
    return 1.0 / (1.0 + jnp.exp(-x))


def _silu(x):
    return x * _sigmoid(x)


def _softplus(x):
    return jnp.maximum(x, 0.0) + jnp.log(1.0 + jnp.exp(-jnp.abs(x)))


def _layernorm(x, g, b):
    mu = jnp.mean(x, axis=-1, keepdims=True)
    xc = x - mu
    var = jnp.mean(xc * xc, axis=-1, keepdims=True)
    return xc * lax.rsqrt(var + EPS) * g + b


def _rmsnorm(x, g):
    return x * lax.rsqrt(jnp.mean(x * x, axis=-1, keepdims=True) + EPS) * g


def _params(sem):
    return pltpu.CompilerParams(dimension_semantics=sem, vmem_limit_bytes=VMEM_LIMIT)


def _inproj_kernel(x_ref, w_ref, o_ref):
    o_ref[...] = _dot(x_ref[...].astype(BF16), w_ref[...])


def _inproj(layer, x2, w):
    n = x2.shape[0]
    return pl.pallas_call(
        _inproj_kernel,
        grid=(n // ROW_TILE,),
        in_specs=[pl.BlockSpec((ROW_TILE, D_MODEL), lambda i: (i, 0)),
                  pl.BlockSpec((None, D_MODEL, PROJ_WIDTH), lambda i: (layer, 0, 0))],
        out_specs=pl.BlockSpec((ROW_TILE, PROJ_WIDTH), lambda i: (i, 0)),
        out_shape=jax.ShapeDtypeStruct((n, PROJ_WIDTH), F32),
        compiler_params=_params(("parallel",)),
        name="inproj",
    )(x2, w)


def _ln2_inproj_kernel(x_ref, f_ref, g_ref, b_ref, w_ref, x2_ref, o_ref):
    x2 = _layernorm(DEEPNORM_ALPHA * x_ref[...] + f_ref[...], g_ref[...], b_ref[...])
    x2_ref[...] = x2
    o_ref[...] = _dot(x2.astype(BF16), w_ref[...])


def _ln2_inproj(layer, x1, ffn, g, b, w):
    n = x1.shape[0]
    row = pl.BlockSpec((ROW_TILE, D_MODEL), lambda i: (i, 0))
    const = pl.BlockSpec((1, D_MODEL), lambda i: (0, 0))
    return pl.pallas_call(
        _ln2_inproj_kernel,
        grid=(n // ROW_TILE,),
        in_specs=[row, row, const, const,
                  pl.BlockSpec((None, D_MODEL, PROJ_WIDTH), lambda i: (layer, 0, 0))],
        out_specs=[row, pl.BlockSpec((ROW_TILE, PROJ_WIDTH), lambda i: (i, 0))],
        out_shape=[jax.ShapeDtypeStruct((n, D_MODEL), F32),
                   jax.ShapeDtypeStruct((n, PROJ_WIDTH), F32)],
        compiler_params=_params(("parallel",)),
        name="ln2_inproj",
    )(x1, ffn, g, b, w)


def _dn_kernel(sm_ref, q_ref, k_ref, v_ref, z_ref, g_ref, wq_ref, wk_ref, wv_ref, nw_ref,
               o_ref, beta_s, gcum_s):
    hp = pl.program_id(1)
    t = q_ref.shape[0]
    c = DN_CHUNK
    n_chunks = t // c
    ii = lax.broadcasted_iota(I32, (c, c), 0)
    jj = lax.broadcasted_iota(I32, (c, c), 1)
    lower_incl = ii >= jj
    cum_mat = jnp.where(ii <= jj, 1.0, 0.0).astype(F32)
    nw = nw_ref[...]

    def conv_silu(x_ref, w_ref, ci, r0, sl):
        w = w_ref[:, sl]
        cur = x_ref[pl.ds(r0, c), sl]
        before = x_ref[pl.ds(pl.multiple_of(jnp.maximum(r0 - 8, 0), 8), 8), sl]
        both = jnp.concatenate([jnp.where(ci > 0, before, 0.0), cur], axis=0)
        y = cur * w[DN_CONV - 1:DN_CONV, :]
        for s in range(1, DN_CONV):
            y = y + both[8 - s:8 - s + c, :] * w[DN_CONV - 1 - s:DN_CONV - s, :]
        return _silu(y)

    def l2norm(x):
        return x * lax.rsqrt(jnp.sum(x * x, axis=-1, keepdims=True) + EPS)

    for hh in range(DN_PAIR):
        sl = slice(hh * LANE, (hh + 1) * LANE)
        head = hp * DN_PAIR + hh
        a_coef = jnp.exp(jnp.full((1, LANE), sm_ref[0, head], F32))
        beta_s[hh] = _sigmoid(g_ref[hh, 0])
        g_log = -a_coef * _softplus(g_ref[hh, 1] + sm_ref[1, head])
        gcum_s[hh] = _dot(g_log, cum_mat, precision=_HI)

    def local(ci):
        r0 = pl.multiple_of(ci * c, c)
        a_pows, sols, outs = [], [], []
        for hh in range(DN_PAIR):
            g_row = jnp.broadcast_to(gcum_s[hh, pl.ds(ci, 1), :], (c, c))
            g_col = g_row.T
            b_col = jnp.broadcast_to(beta_s[hh, pl.ds(ci, 1), :], (c, c)).T
            g_last = jnp.broadcast_to(g_row[:, c - 1:c], (c, c))
            decay = jnp.exp(jnp.where(lower_incl, g_col - g_row, -jnp.inf))
            sl = slice(hh * LANE, (hh + 1) * LANE)
            qc = l2norm(conv_silu(q_ref, wq_ref, ci, r0, sl)) * (DN_DK ** -0.5)
            kc = l2norm(conv_silu(k_ref, wk_ref, ci, r0, sl))
            vc = conv_silu(v_ref, wv_ref, ci, r0, sl)
            kb = kc * b_col
            kcb = kc.astype(BF16)
            a_pows.append(jnp.where(ii > jj, -(_dot_nt(kb.astype(BF16), kcb) * decay), 0.0))
            sols.append(jnp.concatenate([vc * b_col, kb * jnp.exp(g_col)], axis=1))
            outs.append(dict(
                attn=(_dot_nt(qc.astype(BF16), kcb) * decay).astype(BF16),
                k_dec_t=(kc * jnp.exp(g_last - g_col)).T.astype(BF16),
                q_dec=(qc * jnp.exp(g_col)).astype(BF16),
                keep=jnp.exp(g_last)))

        n_steps = (c - 1).bit_length()
        for step in range(n_steps):
            last = step + 1 == n_steps
            rhs = jnp.concatenate(
                [sols[hh] if last else jnp.concatenate([sols[hh], a_pows[hh]], axis=1)
                 for hh in range(DN_PAIR)], axis=0)
            a_bd = jnp.concatenate(
                [jnp.concatenate([a_pows[hh] if g == hh else jnp.zeros((c, c), F32)
                                  for g in range(DN_PAIR)], axis=1)
                 for hh in range(DN_PAIR)], axis=0)
            r_hi, r_lo = _split(rhs)
            both = _dot(a_bd.astype(BF16), jnp.concatenate([r_hi, r_lo], axis=1))
            prod = both[:, 0:rhs.shape[1]] + both[:, rhs.shape[1]:]
            for hh in range(DN_PAIR):
                mine = prod[hh * c:(hh + 1) * c, :]
                sols[hh] = sols[hh] + mine[:, 0:2 * c]
                if not last:
                    a_pows[hh] = mine[:, 2 * c:]
        for hh in range(DN_PAIR):
            outs[hh]["u"] = sols[hh][:, :c]
            outs[hh]["w"] = sols[hh][:, c:].astype(BF16)
        return outs

    def recur(ci, states, loc):
        r0 = pl.multiple_of(ci * c, c)
        new_states = []
        for hh in range(DN_PAIR):
            sl = slice(hh * LANE, (hh + 1) * LANE)
            state = states[hh]
            sb = state.astype(BF16)
            v_new = loc[hh]["u"] - _dot(loc[hh]["w"], sb)
            vb = v_new.astype(BF16)
            o = _dot(loc[hh]["q_dec"], sb) + _dot(loc[hh]["attn"], vb)
            new_states.append(state * loc[hh]["keep"] + _dot(loc[hh]["k_dec_t"], vb))
            o_ref[pl.ds(r0, c), sl] = _rmsnorm(o, nw) * _silu(z_ref[pl.ds(r0, c), sl])
        return tuple(new_states)

    def step(ci, carry):
        states, loc = carry
        return recur(ci - 1, states, loc), local(ci)

    init = tuple(jnp.zeros((c, c), F32) for _ in range(DN_PAIR))
    states, loc = lax.fori_loop(1, n_chunks, step, (init, local(0)), unroll=2)
    recur(n_chunks - 1, states, loc)


def _deltanet(proj, gates, dn_sm, conv_w, norm_w, batch, t):
    n = proj.shape[0]
    nc = t // DN_CHUNK
    width = DN_PAIR * LANE
    n_pairs = DN_HEADS // DN_PAIR
    blk = lambda off: pl.BlockSpec((t, width), lambda b, h: (b, off + h))
    wblk = lambda off: pl.BlockSpec((DN_CONV, width), lambda b, h: (0, off + h))
    per_chunk = pltpu.VMEM((DN_PAIR, nc, LANE), F32)
    return pl.pallas_call(
        _dn_kernel,
        grid=(batch, n_pairs),
        in_specs=[pl.BlockSpec(memory_space=pltpu.SMEM),
                  blk(0), blk(n_pairs), blk(2 * n_pairs), blk(3 * n_pairs),
                  pl.BlockSpec((None, DN_PAIR, 2, nc, LANE), lambda b, h: (b, h, 0, 0, 0)),
                  wblk(0), wblk(n_pairs), wblk(2 * n_pairs),
                  pl.BlockSpec((1, LANE), lambda b, h: (0, 0))],
        out_specs=pl.BlockSpec((t, width), lambda b, h: (b, h)),
        out_shape=jax.ShapeDtypeStruct((n, DN_WIDTH), F32),
        scratch_shapes=[per_chunk, per_chunk],
        compiler_params=_params(("parallel", "parallel")),
        name="deltanet",
    )(dn_sm, proj, proj, proj, proj, gates, conv_w, conv_w, conv_w, norm_w)


def _sa_prep_kernel(cq_ref, ckv_ref, misc_ref, qn_ref, wuq_ref, wuk_ref, wqi_ref, kvn_ref,
                    ig_ref, ib_ref, qlat_ref, qidx_ref, ckvn_ref, kidx_ref):
    cq = _rmsnorm(cq_ref[...], qn_ref[...]).astype(BF16)
    q = _dot(cq, wuq_ref[...])
    qi = _dot(cq, wqi_ref[...])
    for h in range(SA_HEADS):
        qh = q[:, h * SA_DK:(h + 1) * SA_DK].astype(BF16)
        ql = _dot(qh, wuk_ref[h]) * ((SA_DK ** -0.5) * LOG2E)
        qlat_ref[:, h * SA_KV_RANK:(h + 1) * SA_KV_RANK] = ql.astype(BF16)
    for h in range(IDX_HEADS):
        qidx_ref[h] = qi[:, h * IDX_DIM:(h + 1) * IDX_DIM].astype(BF16)
    ckvn_ref[...] = _rmsnorm(ckv_ref[...], kvn_ref[...]).astype(BF16)
    ik = misc_ref[...][:, :IDX_DIM]
    kidx_ref[...] = _layernorm(ik, ig_ref[...], ib_ref[...]).astype(BF16)


def _sa_prep(proj, q_norm, w_uq, w_uk, w_qidx, kv_norm, ig, ib):
    n = proj.shape[0]
    tm = ROW_TILE
    const2 = lambda shp: pl.BlockSpec(shp, lambda i: (0, 0))
    return pl.pallas_call(
        _sa_prep_kernel,
        grid=(n // tm,),
        in_specs=[pl.BlockSpec((tm, SA_Q_RANK), lambda i: (i, COL_CQ // SA_Q_RANK)),
                  pl.BlockSpec((tm, SA_KV_RANK), lambda i: (i, COL_CKV // SA_KV_RANK)),
                  pl.BlockSpec((tm, LANE), lambda i: (i, COL_MISC // LANE)),
                  const2((1, SA_Q_RANK)),
                  const2((SA_Q_RANK, SA_HEADS * SA_DK)),
                  pl.BlockSpec((SA_HEADS, SA_DK, SA_KV_RANK), lambda i: (0, 0, 0)),
                  const2((SA_Q_RANK, IDX_HEADS * IDX_DIM)),
                  const2((1, SA_KV_RANK)),
                  const2((1, IDX_DIM)), const2((1, IDX_DIM))],
        out_specs=[pl.BlockSpec((tm, SA_HEADS * SA_KV_RANK), lambda i: (i, 0)),
                   pl.BlockSpec((IDX_HEADS, tm, IDX_DIM), lambda i: (0, i, 0)),
                   pl.BlockSpec((tm, SA_KV_RANK), lambda i: (i, 0)),
                   pl.BlockSpec((tm, IDX_DIM), lambda i: (i, 0))],
        out_shape=[jax.ShapeDtypeStruct((n, SA_HEADS * SA_KV_RANK), BF16),
                   jax.ShapeDtypeStruct((IDX_HEADS, n, IDX_DIM), BF16),
                   jax.ShapeDtypeStruct((n, SA_KV_RANK), BF16),
                   jax.ShapeDtypeStruct((n, IDX_DIM), BF16)],
        compiler_params=_params(("parallel",)),
        name="sa_prep",
    )(proj, proj, proj, q_norm, w_uq, w_uk, w_qidx, kv_norm, ig, ib)


def _sa_body(j, qidx_ref, qlat_ref, misc_ref, kidx_ref, ckv_ref, o_ref, key_s, tie_s, bound_s,
             *, k_top, t_eff):
    qb = qlat_ref.shape[0]
    kidx = kidx_ref[0:t_eff, :]
    w_rows = misc_ref[...].T * ((IDX_HEADS ** -0.5) * (IDX_DIM ** -0.5))
    logits = jnp.maximum(_dot_nt(kidx, qidx_ref[...].reshape(IDX_HEADS * qb, IDX_DIM)), 0.0)
    score = jnp.zeros((t_eff, qb), F32)
    for h in range(IDX_HEADS):
        score = score + w_rows[MISC_IW + h:MISC_IW + h + 1, :] * logits[:, h * qb:(h + 1) * qb]

    key_pos = lax.broadcasted_iota(I32, (t_eff, qb), 0)
    t_pos = j * qb + lax.broadcasted_iota(I32, (t_eff, qb), 1)
    causal = key_pos <= t_pos
    bits = pltpu.bitcast(score, I32)
    key = bits ^ ((bits >> 31) & 0x7FFFFFFF)
    key_s[0:t_eff, :] = jnp.where(causal, key, INT_MIN)

    def count(flags):
        ways = 8
        accs = [flags[g * 8:(g + 1) * 8, :] for g in range(ways)]
        for r in range(ways, t_eff // 8):
            accs[r % ways] = accs[r % ways] + flags[r * 8:(r + 1) * 8, :]
        while len(accs) > 1:
            accs = [a + b for a, b in zip(accs[0::2], accs[1::2])]
        return jnp.sum(accs[0], axis=0, keepdims=True)

    def thr_step(i, thr_u):
        cand_u = thr_u | lax.shift_left(jnp.int32(1), 31 - i)
        cnt = count(jnp.where(key_s[0:t_eff, :] >= (cand_u ^ INT_MIN), 1.0, 0.0))
        return jnp.where(cnt >= k_top, cand_u, thr_u)

    thr = lax.fori_loop(0, 32, thr_step, jnp.zeros((1, qb), I32), unroll=2) ^ INT_MIN
    key = key_s[0:t_eff, :]
    above = key > thr
    tie = jnp.where(causal, jnp.where(key == thr, 1.0, 0.0), 0.0).astype(F32)
    tie_s[0:t_eff, :] = tie
    need = k_top - count(jnp.where(above, 1.0, 0.0))
    excess = count(tie) - need

    n_bits = t_eff.bit_length()
    bound_s[...] = jnp.full((1, qb), 2 ** n_bits - 1, I32)

    @pl.when(jnp.max(excess) > 0.0)
    def _():
        def tie_step(i, bound):
            cand = bound | lax.shift_left(jnp.int32(1), (n_bits - 1) - i)
            cnt = count(jnp.where(key_pos < cand, tie_s[0:t_eff, :], 0.0))
            return jnp.where(cnt <= need, cand, bound)

        bound_s[...] = lax.fori_loop(0, n_bits, tie_step, jnp.zeros((1, qb), I32))

    chosen = jnp.where(above, 1.0, jnp.where(key_pos < bound_s[...], tie_s[0:t_eff, :], 0.0))
    bias = jnp.where(chosen > 0.0, 0.0, -jnp.inf).astype(F32).T

    ckv = ckv_ref[0:t_eff, :]
    for h in range(SA_HEADS):
        s = _dot_nt(qlat_ref[:, h * SA_KV_RANK:(h + 1) * SA_KV_RANK], ckv) + bias
        m = jnp.max(s, axis=1, keepdims=True)
        p = jnp.exp2(s - m)
        l = jnp.sum(p, axis=1, keepdims=True)
        o = _dot(p.astype(BF16), ckv) / l
        o_ref[:, h * SA_KV_RANK:(h + 1) * SA_KV_RANK] = o.astype(BF16)


def _sa_kernel(qidx_ref, qlat_ref, misc_ref, kidx_ref, ckv_ref, o_ref, key_s, tie_s, bound_s,
               *, k_top, n_cls):
    qb = qlat_ref.shape[0]
    t = kidx_ref.shape[0]
    j = pl.program_id(1)
    per = (t // qb) // n_cls
    for cls in range(n_cls):
        body = functools.partial(_sa_body, j, qidx_ref, qlat_ref, misc_ref, kidx_ref, ckv_ref,
                                 o_ref, key_s, tie_s, bound_s, k_top=k_top,
                                 t_eff=(cls + 1) * per * qb)
        pl.when(j // per == cls)(body)


def _sparse_attention(qidx, qlat, proj, kidx, ckvn, batch, t):
    n = proj.shape[0]
    nq = t // Q_TILE
    k_top = min(TOPK_MAX, t // 4)
    n_cls = SA_CLASSES if nq % SA_CLASSES == 0 else 1
    return pl.pallas_call(
        functools.partial(_sa_kernel, k_top=k_top, n_cls=n_cls),
        grid=(batch, nq),
        in_specs=[pl.BlockSpec((IDX_HEADS, Q_TILE, IDX_DIM), lambda b, j: (0, b * nq + j, 0)),
                  pl.BlockSpec((Q_TILE, SA_HEADS * SA_KV_RANK), lambda b, j: (b * nq + j, 0)),
                  pl.BlockSpec((Q_TILE, LANE), lambda b, j: (b * nq + j, COL_MISC // LANE)),
                  pl.BlockSpec((t, IDX_DIM), lambda b, j: (b, 0)),
                  pl.BlockSpec((t, SA_KV_RANK), lambda b, j: (b, 0))],
        out_specs=pl.BlockSpec((Q_TILE, SA_HEADS * SA_KV_RANK), lambda b, j: (b * nq + j, 0)),
        out_shape=jax.ShapeDtypeStruct((n, SA_HEADS * SA_KV_RANK), BF16),
        scratch_shapes=[pltpu.VMEM((t, Q_TILE), I32), pltpu.VMEM((t, Q_TILE), F32),
                        pltpu.VMEM((1, Q_TILE), I32)],
        compiler_params=_params(("parallel", "parallel")),
        name="sparse_attention",
    )(qidx, qlat, proj, kidx, ckvn)


def _outproj_kernel(ydn_ref, olat_ref, x_ref, wdn_ref, wsa_ref, wuv_ref, g_ref, b_ref, rw_ref,
                    rb_ref, x1_ref, x1b_ref, lg_ref):
    ysa = _dot(olat_ref[...], wuv_ref[...])
    mix = _dot(ydn_ref[...].astype(BF16), wdn_ref[...]) + _dot(ysa.astype(BF16), wsa_ref[...])
    x1 = _layernorm(DEEPNORM_ALPHA * x_ref[...] + mix, g_ref[...], b_ref[...])
    x1_ref[...] = x1
    x_hi, x_lo = _split(x1)
    w_hi, w_lo = _split(rw_ref[...])
    x1b_ref[...] = x_hi
    lg_ref[...] = _dot(x_hi, w_hi) + (_dot(x_hi, w_lo) + _dot(x_lo, w_hi)) + rb_ref[...]


def _outproj(ydn, olat, x2, w_o_dn, w_o_sa, wuv_bd, g, b, rw, rb):
    n = x2.shape[0]
    tm = ROW_TILE
    row = lambda w: pl.BlockSpec((tm, w), lambda i: (i, 0))
    const2 = lambda shp: pl.BlockSpec(shp, lambda i: (0, 0))
    return pl.pallas_call(
        _outproj_kernel,
        grid=(n // tm,),
        in_specs=[row(DN_WIDTH), row(SA_HEADS * SA_KV_RANK), row(D_MODEL),
                  const2((DN_WIDTH, D_MODEL)), const2((SA_WIDTH, D_MODEL)),
                  const2((SA_HEADS * SA_KV_RANK, SA_WIDTH)),
                  const2((1, D_MODEL)), const2((1, D_MODEL)),
                  const2((D_MODEL, N_EXPERTS)), const2((1, N_EXPERTS))],
        out_specs=[row(D_MODEL), row(D_MODEL), row(N_EXPERTS)],
        out_shape=[jax.ShapeDtypeStruct((n, D_MODEL), F32),
                   jax.ShapeDtypeStruct((n, D_MODEL), BF16),
                   jax.ShapeDtypeStruct((n, N_EXPERTS), F32)],
        compiler_params=_params(("parallel",)),
        name="outproj",
    )(ydn, olat, x2, w_o_dn, w_o_sa, wuv_bd, g, b, rw, rb)


def _route_kernel(lg_ref, rt_ref, gt_ref, cnt_ref, pad_s):
    s_tok = lg_ref.shape[0]
    l = lg_ref[...]
    lane = lax.broadcasted_iota(I32, l.shape, 1)
    vals, hots = [], []
    for _ in range(TOP_K):
        m = jnp.max(l, axis=1, keepdims=True)
        idx = jnp.min(jnp.where(l == m, lane, N_EXPERTS), axis=1, keepdims=True)
        hot = lane == idx
        vals.append(m)
        hots.append(hot)
        l = jnp.where(hot, -jnp.inf, l)
    exps = [jnp.exp(v - vals[0]) for v in vals]
    den = exps[0]
    for e in exps[1:]:
        den = den + e
    gates = jnp.zeros(l.shape, F32)
    chosen = jnp.zeros(l.shape, F32)
    for hot, e in zip(hots, exps):
        gates = gates + jnp.where(hot, e / den, 0.0)
        chosen = chosen + jnp.where(hot, 1.0, 0.0)

    blk = 256
    ii = lax.broadcasted_iota(I32, (blk, blk), 0)
    jj = lax.broadcasted_iota(I32, (blk, blk), 1)
    tri = jnp.where(ii > jj, 1.0, 0.0).astype(BF16)
    carry = jnp.zeros((1, N_EXPERTS), F32)
    for ci in range(s_tok // blk):
        mc = chosen[ci * blk:(ci + 1) * blk, :]
        rk = _dot(tri, mc.astype(BF16)) + carry
        rank = jnp.where(mc > 0.0, rk, -1.0)
        pad_s[ci * blk:(ci + 1) * blk, 0:N_EXPERTS] = rank
        pad_s[ci * blk:(ci + 1) * blk, N_EXPERTS:2 * N_EXPERTS] = gates[ci * blk:(ci + 1) * blk, :]
        carry = carry + jnp.sum(mc, axis=0, keepdims=True)
    cnt_ref[...] = carry.astype(I32)
    pad_s[:, 2 * N_EXPERTS:] = jnp.zeros((s_tok, LANE - 2 * N_EXPERTS), F32)
    tr = pad_s[...].T
    rt_ref[...] = tr[0:N_EXPERTS, :].astype(I32)
    gt_ref[...] = tr[N_EXPERTS:2 * N_EXPERTS, :]


def _route(logits, s_tok):
    n = logits.shape[0]
    n_super = n // s_tok
    return pl.pallas_call(
        _route_kernel,
        grid=(n_super,),
        in_specs=[pl.BlockSpec((s_tok, N_EXPERTS), lambda b: (b, 0))],
        out_specs=[pl.BlockSpec((None, N_EXPERTS, s_tok), lambda b: (b, 0, 0)),
                   pl.BlockSpec((None, N_EXPERTS, s_tok), lambda b: (b, 0, 0)),
                   pl.BlockSpec((None, 1, N_EXPERTS), lambda b: (b, 0, 0))],
        out_shape=[jax.ShapeDtypeStruct((n_super, N_EXPERTS, s_tok), I32),
                   jax.ShapeDtypeStruct((n_super, N_EXPERTS, s_tok), F32),
                   jax.ShapeDtypeStruct((n_super, 1, N_EXPERTS), I32)],
        scratch_shapes=[pltpu.VMEM((s_tok, LANE), F32)],
        compiler_params=_params(("parallel",)),
        name="route",
    )(logits)


def _moe_kernel(cnt_ref, x_ref, rt_ref, gt_ref, wgu_ref, bgu_ref, wdn_ref, bdn_ref, o_ref):
    b = pl.program_id(0)
    e = pl.program_id(1)
    group, _, s_tok = rt_ref.shape
    m = MOE_ROWS

    @pl.when(e == 0)
    def _():
        o_ref[...] = jnp.zeros(o_ref.shape, F32)

    sub_iota = lax.broadcasted_iota(I32, (MXU_DIM, s_tok), 0)
    pad_rows = jnp.zeros((MXU_DIM - m, D_MODEL), BF16)

    for g in range(group):
        rows = slice(g * s_tok, (g + 1) * s_tok)
        cnt = cnt_ref[(b * group + g) * N_EXPERTS + e]
        rank_row = rt_ref[g, pl.ds(e, 1), :]
        gate_row = gt_ref[g, pl.ds(e, 1), :]

        def one_pass(p, carry, rows=rows, rank_row=rank_row, gate_row=gate_row):
            hit = (rank_row - p * m) == sub_iota
            pick = jnp.where(hit, 1.0, 0.0).astype(BF16)
            xs = _dot(pick[0:m, :], x_ref[rows, :]).astype(BF16)
            hgu = _dot(xs, wgu_ref[...]) + bgu_ref[...]
            gate = jnp.minimum(hgu[:, :D_FF], SWIGLU_LIMIT)
            up = jnp.clip(hgu[:, D_FF:], -SWIGLU_LIMIT, SWIGLU_LIMIT)
            hdn = (up + 1.0) * (gate * _sigmoid(SWIGLU_ALPHA * gate))
            y = _dot(hdn.astype(BF16), wdn_ref[...]) + bdn_ref[...]
            g_rows = jnp.sum(jnp.where(hit[0:m, :], gate_row, 0.0), axis=1, keepdims=True)
            ys = jnp.concatenate([(y * g_rows).astype(BF16), pad_rows], axis=0)
            o_ref[rows, :] += _dot_tn(pick, ys)
            return carry

        lax.fori_loop(0, (cnt + m - 1) // m, one_pass, 0)


def _moe(layer, counts, x1b, rt, gt, wgu, bgu, wdn, bdn):
    n = x1b.shape[0]
    n_super, _, s_tok = rt.shape
    group = MOE_GROUP if n_super % MOE_GROUP == 0 else 1
    return pl.pallas_call(
        _moe_kernel,
        grid_spec=pltpu.PrefetchScalarGridSpec(
            num_scalar_prefetch=1,
            grid=(n_super // group, N_EXPERTS),
            in_specs=[pl.BlockSpec((group * s_tok, D_MODEL), lambda b, e, c: (b, 0)),
                      pl.BlockSpec((group, N_EXPERTS, s_tok), lambda b, e, c: (b, 0, 0)),
                      pl.BlockSpec((group, N_EXPERTS, s_tok), lambda b, e, c: (b, 0, 0)),
                      pl.BlockSpec((None, None, D_MODEL, 2 * D_FF),
                                   lambda b, e, c: (layer, e, 0, 0)),
                      pl.BlockSpec((None, None, 1, 2 * D_FF), lambda b, e, c: (layer, e, 0, 0)),
                      pl.BlockSpec((None, None, D_FF, D_MODEL), lambda b, e, c: (layer, e, 0, 0)),
                      pl.BlockSpec((None, None, 1, D_MODEL), lambda b, e, c: (layer, e, 0, 0))],
            out_specs=pl.BlockSpec((group * s_tok, D_MODEL), lambda b, e, c: (b, 0))),
        out_shape=jax.ShapeDtypeStruct((n, D_MODEL), F32),
        compiler_params=_params(("parallel", "arbitrary")),
        name="moe",
    )(counts, x1b, rt, gt, wgu, bgu, wdn, bdn)


def _ln2_kernel(x_ref, f_ref, g_ref, b_ref, o_ref):
    o_ref[...] = _layernorm(DEEPNORM_ALPHA * x_ref[...] + f_ref[...], g_ref[...], b_ref[...])


def _ln2(x1, ffn, g, b):
    n = x1.shape[0]
    tm = ROW_TILE
    row = pl.BlockSpec((tm, D_MODEL), lambda i: (i, 0))
    const = pl.BlockSpec((1, D_MODEL), lambda i: (0, 0))
    return pl.pallas_call(
        _ln2_kernel,
        grid=(n // tm,),
        in_specs=[row, row, const, const],
        out_specs=row,
        out_shape=jax.ShapeDtypeStruct((n, D_MODEL), F32),
        compiler_params=_params(("parallel",)),
        name="ln2",
    )(x1, ffn, g, b)


def _permute_w_in(w_in):
    depth = w_in.shape[0]
    o_db = 4 * DN_QK
    o_da = o_db + DN_HEADS
    o_cq = o_da + DN_HEADS
    o_ckv = o_cq + SA_Q_RANK
    o_ik = o_ckv + SA_KV_RANK
    o_iw = o_ik + IDX_DIM
    o_end = o_iw + IDX_HEADS
    parts = [w_in[..., :o_db], w_in[..., o_cq:o_ckv], w_in[..., o_ckv:o_ik], w_in[..., o_ik:o_iw],
             w_in[..., o_iw:o_end], w_in[..., o_db:o_da], w_in[..., o_da:o_cq]]
    width = sum(p.shape[-1] for p in parts)
    parts.append(jnp.zeros((depth, D_MODEL, PROJ_WIDTH - width), w_in.dtype))
    return jnp.concatenate(parts, axis=-1).astype(BF16)


def kernel(x, w_in, dn_conv, dn_a_log, dn_dt_bias, dn_norm, sa_q_norm, sa_w_uq, sa_kv_norm, sa_w_uk, sa_w_uv, idx_w_q, idx_k_norm_g, idx_k_norm_b, w_o, ln1_g, ln1_b, router_w, router_b, w_gate_up, b_gate_up, w_down, b_down, ln2_g, ln2_b):
    batch, t, d = x.shape
    depth = w_in.shape[0]
    n = batch * t
    nc = t // DN_CHUNK
    s_moe = min(MOE_TOKENS, n)

    w_in_p = _permute_w_in(w_in)
    w_uq = sa_w_uq.astype(BF16)
    w_uk = sa_w_uk.astype(BF16)
    w_qidx = idx_w_q.astype(BF16)
    w_o_b = w_o.astype(BF16)
    eye_h = jnp.eye(SA_HEADS, dtype=F32)
    wuv_bd = jnp.einsum('lhrv,hg->lhrgv', sa_w_uv, eye_h).reshape(
        depth, SA_HEADS * SA_KV_RANK, SA_WIDTH).astype(BF16)
    wgu = w_gate_up.astype(BF16)
    wdn = w_down.astype(BF16)
    dn_sm = jnp.stack([dn_a_log, dn_dt_bias], axis=1)

    x2 = x.reshape(n, d)
    x1 = ffn = None
    for l in range(depth):
        if l == 0:
            proj = _inproj(l, x2, w_in_p)
        else:
            x2, proj = _ln2_inproj(l, x1, ffn, ln2_g[l - 1][None, :], ln2_b[l - 1][None, :], w_in_p)
        gates = proj[:, COL_MISC + MISC_DB:COL_MISC + MISC_DA + DN_HEADS]
        gates = gates.reshape(batch, nc, DN_CHUNK, 2, DN_HEADS).transpose(0, 4, 3, 1, 2)
        ydn = _deltanet(proj, gates, dn_sm[l], dn_conv[l], dn_norm[l][None, :], batch, t)
        qlat, qidx, ckvn, kidx = _sa_prep(
            proj, sa_q_norm[l][None, :], w_uq[l], w_uk[l], w_qidx[l], sa_kv_norm[l][None, :],
            idx_k_norm_g[l][None, :], idx_k_norm_b[l][None, :])
        olat = _sparse_attention(qidx, qlat, proj, kidx, ckvn, batch, t)
        x1, x1b, logits = _outproj(
            ydn, olat, x2, w_o_b[l, :DN_WIDTH], w_o_b[l, DN_WIDTH:], wuv_bd[l],
            ln1_g[l][None, :], ln1_b[l][None, :], router_w[l], router_b[l][None, :])
        rt, gt, cnt = _route(logits, s_moe)
        ffn = _moe(l, cnt.reshape(-1), x1b, rt, gt, wgu, b_gate_up[:, :, None, :], wdn,
                   b_down[:, :, None, :])
    x2 = _ln2(x1, ffn, ln2_g[depth - 1][None, :], ln2_b[depth - 1][None, :])
    return x2.reshape(batch, t, d)
```
